```python
import functools
import jax, jax.numpy as jnp
from jax import lax
import numpy as np

D_MODEL = 2048
BATCH = 32
SEQ = 256
DEPTH = 4
DEC_BATCH = 8
DEC_SEQ = 2048
PAST_LEN = 512

GRID_W = 64
H_A = 8
DK_A = 128
DV_A = 128
A_W = H_A * DK_A
CHUNK_A = 32
B_GROUPS = 8
B_W = 1024
B_GD = B_W // B_GROUPS
CHUNK_B = 128
C_W = 1024
CONV_K = 3
H_D = 8
HD_D = 128
D_W = H_D * HD_D
WIN_R = 8
WIN_C = 16
CTX_Q_BLOCK = 128
D_FF = 7168
N_EXPERTS = 8
TOP_K = 2
N_EVEN = (DEPTH + 1) // 2
N_ODD = DEPTH // 2
ALPHA = (2 * DEPTH) ** 0.25
BETA = (8 * DEPTH) ** -0.25
LN_EPS = 1e-5
RMS_EPS = 1e-6
EVEN_SPLITS = (A_W, 2 * A_W, 3 * A_W, 4 * A_W, 5 * A_W, 5 * A_W + B_W)
EVEN_IN = 5 * A_W + 2 * B_W
ODD_SPLITS = (C_W, 2 * C_W, 3 * C_W, 3 * C_W + D_W, 3 * C_W + 2 * D_W)
ODD_IN = 3 * C_W + 3 * D_W
EVEN_MIX = A_W + B_W
ODD_MIX = C_W + D_W

kernel_name = 'hybrid_diffusion_hgrn2_natten_step'


def layer_norm(x, g, b):
    xf = x.astype(jnp.float32)
    mu = jnp.mean(xf, -1, keepdims=True)
    xc = xf - mu
    var = jnp.mean(xc * xc, -1, keepdims=True)
    return (xc * lax.rsqrt(var + LN_EPS) * g.astype(jnp.float32) + b.astype(jnp.float32)).astype(x.dtype)


def split_heads(a, nh):
    bsz, t, w = a.shape
    return a.reshape(bsz, t, nh, w // nh).transpose(0, 2, 1, 3)


def merge_heads(a):
    bsz, nh, t, d = a.shape
    return a.transpose(0, 2, 1, 3).reshape(bsz, t, nh * d)


def hgrn_lower_bounds(lb_logits):
    p = jax.nn.softmax(lb_logits.astype(jnp.float32), axis=1)
    cs = jnp.cumsum(p, axis=1)
    return cs - cs[:, :1]


def hgrn2_forget(raw, lb):
    raw = split_heads(raw, H_A).astype(jnp.float32)
    lb = lb.astype(jnp.float32).reshape(H_A, 1, DK_A)
    log_f = jnp.logaddexp(jnp.log(lb), jnp.log1p(-lb) + jax.nn.log_sigmoid(raw))
    return log_f, (1.0 - lb) * jax.nn.sigmoid(-raw)


def hgrn2_chunk_scan(q, k, v, log_f, s0):
    bsz, nh, t, dk = q.shape
    n = t // CHUNK_A

    def to_chunks(a):
        return a.reshape(bsz, nh, n, CHUNK_A, a.shape[-1]).transpose(2, 0, 1, 3, 4)

    incl = jnp.tril(jnp.ones((CHUNK_A, CHUNK_A), dtype=bool))[:, :, None]

    def step(S, inp):
        qc, kc, vc, gc = inp
        bc = jnp.cumsum(gc, axis=2)
        decay = jnp.exp(jnp.where(incl, bc[:, :, :, None, :] - bc[:, :, None, :, :], -jnp.inf))
        attn = jnp.einsum('bhtk,bhsk,bhtsk->bhts', qc, kc, decay)
        o = jnp.einsum('bhtk,bhkv->bhtv', qc * jnp.exp(bc), S) + jnp.einsum('bhts,bhsv->bhtv', attn, vc)
        b_end = bc[:, :, -1, :]
        S = jnp.exp(b_end)[..., None] * S + jnp.einsum('bhsk,bhsv->bhkv', kc * jnp.exp(b_end[:, :, None, :] - bc), vc)
        return S, o

    s_fin, o = lax.scan(step, s0, (to_chunks(q), to_chunks(k), to_chunks(v), to_chunks(log_f)))
    return o.transpose(1, 2, 0, 3, 4).reshape(bsz, nh, t, v.shape[-1]), s_fin


def spatial_gating(u, v, w_s, b_s, ln_g, ln_b):
    bsz, t, _ = v.shape
    u = jax.nn.gelu(u)
    v = layer_norm(jax.nn.gelu(v), ln_g, ln_b)
    vg = v.reshape(bsz, t // CHUNK_B, CHUNK_B, B_GROUPS, B_GD)
    mixed = jnp.einsum('gts,bnsgd->bntgd', w_s, vg) + b_s.T[:, :, None]
    return u * mixed.reshape(bsz, t, B_W)


def even_mixer(h, state0, w_in, w_out, lb_f, lb_b, gnorm_w, w_s, b_s, sgu_ln_g, sgu_ln_b):
    bsz, t, _ = h.shape
    q, f_fw, f_bw, i_in, g_out, u, v = jnp.split(h @ w_in, EVEN_SPLITS, axis=-1)
    qh = jax.nn.silu(split_heads(q, H_A).astype(jnp.float32))
    vh = split_heads(i_in, H_A).astype(jnp.float32)
    logf_fw, k_fw = hgrn2_forget(f_fw, lb_f)
    logf_bw, k_bw = hgrn2_forget(f_bw, lb_b)
    if state0 is None:
        s_fw0 = jnp.zeros((bsz, H_A, DK_A, DV_A), jnp.float32)
        s_bw0 = s_fw0
    else:
        s_fw0, s_bw0 = [s.astype(jnp.float32) for s in state0]
    o_fw, s_fw = hgrn2_chunk_scan(qh, k_fw, vh, logf_fw, s_fw0)
    rev = lambda a: jnp.flip(a, axis=2)
    o_bw, s_bw = hgrn2_chunk_scan(rev(qh), rev(k_bw), rev(vh), rev(logf_bw), s_bw0)
    o = o_fw + rev(o_bw)
    o = o * lax.rsqrt(jnp.mean(o * o, -1, keepdims=True) + RMS_EPS) * gnorm_w.astype(jnp.float32)
    o = o * jax.nn.silu(split_heads(g_out, H_A).astype(jnp.float32))
    y_a = merge_heads(o).astype(h.dtype)
    y_b = spatial_gating(u, v, w_s, b_s, sgu_ln_g, sgu_ln_b)
    out = jnp.concatenate([y_a, y_b], axis=-1) @ w_out
    new = (s_fw.astype(h.dtype), s_bw.astype(h.dtype)) if state0 is None else ()
    return out, new


def short_conv(b_gate, c_gate, hc, conv_w):
    z = c_gate * hc
    y = lax.conv_general_dilated(z, conv_w[:, None, :], window_strides=(1,), padding=((CONV_K // 2, CONV_K // 2),),
                                 dimension_numbers=('NWC', 'WIO', 'NWC'), feature_group_count=C_W)
    return b_gate * y


def context_attention(q, k, v):
    bsz, nh, L, d = q.shape
    nb = L // CTX_Q_BLOCK
    scale = d ** -0.5
    qb = q.reshape(bsz, nh, nb, CTX_Q_BLOCK, d).transpose(2, 0, 1, 3, 4)

    def block(qi):
        s = jnp.einsum('bhqd,bhkd->bhqk', qi, k).astype(jnp.float32) * scale
        p = jax.nn.softmax(s, axis=-1).astype(v.dtype)
        return jnp.einsum('bhqk,bhkd->bhqd', p, v)

    return lax.map(block, qb).transpose(1, 2, 0, 3, 4).reshape(bsz, nh, L, d)


def neighbourhood_attention(q, k, v, k_ctx, v_ctx, rpb):
    bsz, nh, t, d = q.shape
    rows = t // GRID_W
    wr = min(WIN_R, rows)
    scale = d ** -0.5
    qg = q.reshape(bsz, nh, rows, GRID_W, d)
    kg = k.reshape(bsz, nh, rows, GRID_W, d)
    vg = v.reshape(bsz, nh, rows, GRID_W, d)
    col = np.arange(GRID_W)
    cs = np.clip(col - WIN_C // 2, 0, GRID_W - WIN_C)
    col_ok = (col[None, :] >= cs[:, None]) & (col[None, :] < cs[:, None] + WIN_C)
    dc = np.clip(col[None, :] - col[:, None] + WIN_C - 1, 0, 2 * WIN_C - 2)
    mask = jnp.asarray(col_ok)[None, None, :, None, :]

    def row_block(r):
        rs = jnp.clip(r - wr // 2, 0, rows - wr)
        kb = lax.dynamic_slice_in_dim(kg, rs, wr, axis=2)
        vb = lax.dynamic_slice_in_dim(vg, rs, wr, axis=2)
        qr = lax.dynamic_index_in_dim(qg, r, axis=2, keepdims=False)
        dr = rs + jnp.arange(wr) - r + WIN_R - 1
        bias = rpb[:, dr][:, :, dc].transpose(0, 2, 1, 3).astype(jnp.float32)
        s_loc = jnp.einsum('bhqd,bhrkd->bhqrk', qr, kb).astype(jnp.float32) * scale + bias
        s_loc = jnp.where(mask, s_loc, -jnp.inf).reshape(bsz, nh, GRID_W, wr * GRID_W)
        s_ctx = jnp.einsum('bhqd,bhld->bhql', qr, k_ctx).astype(jnp.float32) * scale
        p = jax.nn.softmax(jnp.concatenate([s_loc, s_ctx], axis=-1), axis=-1).astype(v.dtype)
        p_loc = p[..., :wr * GRID_W].reshape(bsz, nh, GRID_W, wr, GRID_W)
        return (jnp.einsum('bhqrk,bhrkd->bhqd', p_loc, vb)
                + jnp.einsum('bhql,bhld->bhqd', p[..., wr * GRID_W:], v_ctx))

    out = lax.map(row_block, jnp.arange(rows))
    return out.transpose(1, 2, 0, 3, 4).reshape(bsz, nh, t, d)


def odd_mixer(h, state0, w_in, w_out, conv_w, rpb):
    b_gate, c_gate, hc, q, k, v = jnp.split(h @ w_in, ODD_SPLITS, axis=-1)
    y_c = short_conv(b_gate, c_gate, hc, conv_w)
    qh, kh, vh = [split_heads(a, H_D) for a in (q, k, v)]
    if state0 is None:
        o = context_attention(qh, kh, vh)
        new = (kh, vh)
    else:
        o = neighbourhood_attention(qh, kh, vh, state0[0], state0[1], rpb)
        new = ()
    out = jnp.concatenate([y_c, merge_heads(o)], axis=-1) @ w_out
    return out, new


def swiglu(h, w_gu, w_down):
    g, u = jnp.split(h @ w_gu, 2, axis=-1)
    return (jax.nn.silu(g) * u) @ w_down


def moe_swiglu(h, router, w_gu, w_down):
    bsz, t, dm = h.shape
    ht = h.reshape(bsz * t, dm)
    logits = (ht @ router).astype(jnp.float32)
    top_val, top_idx = lax.top_k(logits, TOP_K)
    gate = jnp.sum(jax.nn.one_hot(top_idx, N_EXPERTS, dtype=jnp.float32)
                   * jax.nn.softmax(top_val, axis=-1)[..., None], axis=1).astype(h.dtype)
    out = jnp.zeros_like(ht)
    for e in range(N_EXPERTS):
        out = out + gate[:, e:e + 1] * swiglu(ht, w_gu[e], w_down[e])
    return out.reshape(bsz, t, dm)


def trunk_block(x, cond, state0, mixer, ffn, w_mod, b_mod, ln_g, ln_b):
    m = (jax.nn.silu(cond) @ w_mod + b_mod)[:, None, :]
    sh1, sc1, g1, sh2, sc2, g2 = jnp.split(m, 6, axis=-1)
    mix, state = mixer(x * (1 + sc1) + sh1, state0)
    x = layer_norm(ALPHA * x + g1 * mix, ln_g[0], ln_b[0])
    x = layer_norm(ALPHA * x + g2 * ffn(x * (1 + sc2) + sh2), ln_g[1], ln_b[1])
    return x, state


def setup_inputs(seed: int = 0) -> dict:
    key = jax.random.key(seed)
    keys = iter(jax.random.split(key, 40))

    def nrm(shape, s=1.0):
        return jax.random.normal(next(keys), shape, jnp.float32) * s

    return dict(
        x_prompt=nrm((BATCH, SEQ, D_MODEL)),
        x_sample=nrm((DEC_BATCH, DEC_SEQ, D_MODEL)),
        state_fwd_0=nrm((DEC_BATCH, H_A, DK_A, DV_A)),
        state_bwd_0=nrm((DEC_BATCH, H_A, DK_A, DV_A)),
        cache_k_1=nrm((DEC_BATCH, H_D, PAST_LEN, HD_D)),
        cache_v_1=nrm((DEC_BATCH, H_D, PAST_LEN, HD_D)),
        state_fwd_2=nrm((DEC_BATCH, H_A, DK_A, DV_A)),
        state_bwd_2=nrm((DEC_BATCH, H_A, DK_A, DV_A)),
        cache_k_3=nrm((DEC_BATCH, H_D, PAST_LEN, HD_D)),
        cache_v_3=nrm((DEC_BATCH, H_D, PAST_LEN, HD_D)),
        c=nrm((DEC_BATCH, D_MODEL)),
        c_ctx=nrm((D_MODEL,)),
        w_mod=nrm((DEPTH, D_MODEL, 6 * D_MODEL), 0.5 * D_MODEL ** -0.5),
        b_mod=nrm((DEPTH, 6 * D_MODEL), 0.02),
        ln_g=1.0 + nrm((DEPTH, 2, D_MODEL), 0.02),
        ln_b=nrm((DEPTH, 2, D_MODEL), 0.02),
        w_in_even=nrm((N_EVEN, D_MODEL, EVEN_IN), D_MODEL ** -0.5),
        w_out_even=nrm((N_EVEN, EVEN_MIX, D_MODEL), BETA * EVEN_MIX ** -0.5),
        hgrn_lb_logits=nrm((2, N_EVEN, A_W), 0.5),
        hgrn_gnorm=1.0 + nrm((N_EVEN, DV_A), 0.02),
        sgu_w=nrm((N_EVEN, B_GROUPS, CHUNK_B, CHUNK_B), CHUNK_B ** -0.5),
        sgu_b=1.0 + nrm((N_EVEN, B_GROUPS, CHUNK_B), 0.02),
        sgu_ln_g=1.0 + nrm((N_EVEN, B_W), 0.02),
        sgu_ln_b=nrm((N_EVEN, B_W), 0.02),
        ffn_w_gu=nrm((N_EVEN, D_MODEL, 2 * D_FF), D_MODEL ** -0.5),
        ffn_w_down=nrm((N_EVEN, D_FF, D_MODEL), BETA * D_FF ** -0.5),
        w_in_odd=nrm((N_ODD, D_MODEL, ODD_IN), D_MODEL ** -0.5),
        w_out_odd=nrm((N_ODD, ODD_MIX, D_MODEL), BETA * ODD_MIX ** -0.5),
        conv_w=nrm((N_ODD, CONV_K, C_W), CONV_K ** -0.5),
        na_rpb=nrm((N_ODD, H_D, 2 * WIN_R - 1, 2 * WIN_C - 1), 0.1),
        moe_router=nrm((N_ODD, D_MODEL, N_EXPERTS), D_MODEL ** -0.5),
        moe_w_gu=nrm((N_ODD, N_EXPERTS, D_MODEL, 2 * D_FF), D_MODEL ** -0.5),
        moe_w_down=nrm((N_ODD, N_EXPERTS, D_FF, D_MODEL), BETA * D_FF ** -0.5),
    )


def reference(x_prompt, x_sample, state_fwd_0, state_bwd_0, cache_k_1, cache_v_1, state_fwd_2, state_bwd_2,
              cache_k_3, cache_v_3, c, c_ctx, w_mod, b_mod, ln_g, ln_b, w_in_even, w_out_even, hgrn_lb_logits,
              hgrn_gnorm, sgu_w, sgu_b, sgu_ln_g, sgu_ln_b, ffn_w_gu, ffn_w_down, w_in_odd, w_out_odd, conv_w,
              na_rpb, moe_router, moe_w_gu, moe_w_down):
    lb = hgrn_lower_bounds(hgrn_lb_logits)
    cached = [(state_fwd_0, state_bwd_0), (cache_k_1, cache_v_1), (state_fwd_2, state_bwd_2), (cache_k_3, cache_v_3)]
    cond_ctx = c_ctx[None, :]
    xp, xs = x_prompt, x_sample
    new_state = []
    for l in range(DEPTH):
        j = l // 2
        if l % 2 == 0:
            mixer = functools.partial(even_mixer, w_in=w_in_even[j], w_out=w_out_even[j], lb_f=lb[0, j],
                                      lb_b=lb[1, j], gnorm_w=hgrn_gnorm[j], w_s=sgu_w[j], b_s=sgu_b[j],
                                      sgu_ln_g=sgu_ln_g[j], sgu_ln_b=sgu_ln_b[j])
            ffn = functools.partial(swiglu, w_gu=ffn_w_gu[j], w_down=ffn_w_down[j])
        else:
            mixer = functools.partial(odd_mixer, w_in=w_in_odd[j], w_out=w_out_odd[j], conv_w=conv_w[j],
                                      rpb=na_rpb[j])
            ffn = functools.partial(moe_swiglu, router=moe_router[j], w_gu=moe_w_gu[j], w_down=moe_w_down[j])
        xp, st = trunk_block(xp, cond_ctx, None, mixer, ffn, w_mod[l], b_mod[l], ln_g[l], ln_b[l])
        xs, _ = trunk_block(xs, c, cached[l], mixer, ffn, w_mod[l], b_mod[l], ln_g[l], ln_b[l])
        new_state.extend(st)
    return (xp, xs, *new_state)
```

```python
import functools
from typing import NamedTuple

import numpy as np
import jax
import jax.numpy as jnp
from jax import lax
from jax.experimental import pallas as pl
from jax.experimental.pallas import tpu as pltpu

F32 = jnp.float32
BF16 = jnp.bfloat16
HIGHEST = lax.Precision.HIGHEST

D_MODEL = 2048
DEPTH = 4
GRID_W = 64
H_A = 8
DK_A = 128
A_W = H_A * DK_A
B_GROUPS = 8
B_W = 1024
CHUNK_B = 128
C_W = 1024
H_D = 8
HD_D = 128
D_W = H_D * HD_D
WIN_R = 8
WIN_C = 16
D_FF = 7168
N_EXPERTS = 8
ALPHA = (2 * DEPTH) ** 0.25
LN_EPS = 1e-5
RMS_EPS = 1e-6

MOD_ROWS = 16
SCAN_CHUNK = 16
SCAN_BLOCK = 256
ROW_TILE = 256
NEG_BIG = -1e30
VMEM_LIMIT = 56 * 1024 * 1024


class Dims(NamedTuple):
    bp: int
    tp: int
    bs: int
    ts: int

    @property
    def n_prompt(self):
        return self.bp * self.tp

    @property
    def n_sample(self):
        return self.bs * self.ts

    @property
    def n(self):
        return self.bp * self.tp + self.bs * self.ts


def _params(sem, vmem=VMEM_LIMIT):
    return pltpu.CompilerParams(dimension_semantics=sem, vmem_limit_bytes=vmem)


def _sigmoid(x):
    return 1.0 / (1.0 + jnp.exp(-x))


def _silu(x):
    return x * _sigmoid(x)


def _gelu_tanh(x):
    return 0.5 * x * (1.0 + jnp.tanh(np.sqrt(2.0 / np.pi).astype(np.float32) * (x + 0.044715 * (x * x * x))))


def _mod_row(tile, tm, dims):
    r0 = tile * tm
    return jnp.where(r0 < dims.n_prompt, 0, 1 + (r0 - dims.n_prompt) // dims.ts)


def _mod_spec(layer, which, nargs):
    if nargs == 1:
        return pl.BlockSpec((None, None, MOD_ROWS, D_MODEL), lambda i: (layer, which, 0, 0))
    return pl.BlockSpec((None, None, MOD_ROWS, D_MODEL), lambda i, j: (layer, which, 0, 0))


def _mod_kernel(cond_ref, w_ref, b_ref, o_ref):
    c = cond_ref[...]
    s = _silu(c).astype(BF16)
    o_ref[...] = jnp.dot(s, w_ref[...].astype(BF16), preferred_element_type=F32) + b_ref[...]


def modulation_table(cond, w_mod, b_mod):
    depth, d, _ = w_mod.shape
    tn = 1024
    nb = d // tn
    return pl.pallas_call(
        _mod_kernel,
        grid=(depth, 6, nb),
        in_specs=[pl.BlockSpec((MOD_ROWS, d), lambda l, w, n: (0, 0)),
                  pl.BlockSpec((None, d, tn), lambda l, w, n: (l, 0, w * nb + n)),
                  pl.BlockSpec((None, 1, tn), lambda l, w, n: (l, 0, w * nb + n))],
        out_specs=pl.BlockSpec((None, None, MOD_ROWS, tn), lambda l, w, n: (l, w, 0, n)),
        out_shape=jax.ShapeDtypeStruct((depth, 6, MOD_ROWS, d), F32),
        compiler_params=_params(("parallel", "parallel", "parallel")),
        name="modulation",
    )(cond, w_mod, b_mod.reshape(depth, 1, 6 * d))


def _premod_kernel(x_ref, sc_ref, sh_ref, o_ref, *, dims, tm):
    r = _mod_row(pl.program_id(0), tm, dims)
    o_ref[...] = (x_ref[...] * (1.0 + sc_ref[pl.ds(r, 1), :]) + sh_ref[pl.ds(r, 1), :]).astype(o_ref.dtype)


def premodulate(x, mod, layer, dims):
    n, d = x.shape
    tm = 256
    return pl.pallas_call(
        functools.partial(_premod_kernel, dims=dims, tm=tm),
        grid=(n // tm,),
        in_specs=[pl.BlockSpec((tm, d), lambda i: (i, 0)), _mod_spec(layer, 1, 1), _mod_spec(layer, 0, 1)],
        out_specs=pl.BlockSpec((tm, d), lambda i: (i, 0)),
        out_shape=jax.ShapeDtypeStruct((n, d), BF16),
        compiler_params=_params(("parallel",)),
        name="premod",
    )(x, mod, mod)


def _mm_kernel(x_ref, w_ref, o_ref):
    o_ref[...] = jnp.dot(x_ref[...], w_ref[...], preferred_element_type=F32).astype(o_ref.dtype)


def matmul(x, w, out_dtype=F32, tm=1024, tn=1024):
    n, k = x.shape
    m = w.shape[1]
    while n % tm:
        tm //= 2
    return pl.pallas_call(
        _mm_kernel,
        grid=(n // tm, m // tn),
        in_specs=[pl.BlockSpec((tm, k), lambda i, j: (i, 0)), pl.BlockSpec((k, tn), lambda i, j: (0, j))],
        out_specs=pl.BlockSpec((tm, tn), lambda i, j: (i, j)),
        out_shape=jax.ShapeDtypeStruct((n, m), out_dtype),
        compiler_params=_params(("parallel", "parallel")),
        name="matmul",
    )(x, w)


def _gu_kernel(te_ref, tv_ref, x_ref, wg_ref, wu_ref, o_ref):
    del te_ref

    @pl.when(tv_ref[pl.program_id(0)] != 0)
    def _():
        x = x_ref[...]
        g = jnp.dot(x, wg_ref[...], preferred_element_type=F32)
        u = jnp.dot(x, wu_ref[...], preferred_element_type=F32)
        o_ref[...] = (_silu(g) * u).astype(o_ref.dtype)

    @pl.when(tv_ref[pl.program_id(0)] == 0)
    def _():
        o_ref[...] = jnp.zeros_like(o_ref)


def grouped_gate_up(x, w_gu, tile_expert, tile_valid, tm, tn=1024):
    p, d = x.shape
    f = w_gu.shape[2] // 2
    nj = f // tn
    grid_spec = pltpu.PrefetchScalarGridSpec(
        num_scalar_prefetch=2,
        grid=(p // tm, nj),
        in_specs=[pl.BlockSpec((tm, d), lambda i, j, te, tv: (i, 0)),
                  pl.BlockSpec((None, d, tn), lambda i, j, te, tv: (te[i], 0, j)),
                  pl.BlockSpec((None, d, tn), lambda i, j, te, tv: (te[i], 0, nj + j))],
        out_specs=pl.BlockSpec((tm, tn), lambda i, j, te, tv: (i, j)),
    )
    return pl.pallas_call(
        _gu_kernel,
        grid_spec=grid_spec,
        out_shape=jax.ShapeDtypeStruct((p, f), BF16),
        compiler_params=_params(("parallel", "parallel")),
        name="gate_up",
    )(tile_expert, tile_valid, x, w_gu, w_gu)


def _down_kernel(te_ref, tv_ref, h_ref, w_ref, o_ref):
    del te_ref
    k = pl.program_id(1)
    valid = tv_ref[pl.program_id(0)] != 0

    @pl.when(k == 0)
    def _():
        o_ref[...] = jnp.zeros_like(o_ref)

    @pl.when(valid)
    def _():
        o_ref[...] += jnp.dot(h_ref[...], w_ref[...], preferred_element_type=F32)


def grouped_down(h, w_down, tile_expert, tile_valid, tm, tk=1024):
    p, f = h.shape
    d = w_down.shape[2]
    grid_spec = pltpu.PrefetchScalarGridSpec(
        num_scalar_prefetch=2,
        grid=(p // tm, f // tk),
        in_specs=[pl.BlockSpec((tm, tk), lambda i, k, te, tv: (i, k)),
                  pl.BlockSpec((None, tk, d), lambda i, k, te, tv: (te[i], k, 0))],
        out_specs=pl.BlockSpec((tm, d), lambda i, k, te, tv: (i, 0)),
    )
    return pl.pallas_call(
        _down_kernel,
        grid_spec=grid_spec,
        out_shape=jax.ShapeDtypeStruct((p, d), F32),
        compiler_params=_params(("parallel", "arbitrary")),
        name="down",
    )(tile_expert, tile_valid, h, w_down)


def _resid_ln_kernel(*refs, dims, tm, emit_next):
    if emit_next:
        y_ref, x_ref, g_ref, lng_ref, lnb_ref, scn_ref, shn_ref, xo_ref, xm_ref = refs
    else:
        y_ref, x_ref, g_ref, lng_ref, lnb_ref, xo_ref = refs
    r = _mod_row(pl.program_id(0), tm, dims)
    z = ALPHA * x_ref[...] + g_ref[pl.ds(r, 1), :] * y_ref[...]
    mu = jnp.mean(z, axis=-1, keepdims=True)
    zc = z - mu
    var = jnp.mean(zc * zc, axis=-1, keepdims=True)
    xn = zc * lax.rsqrt(var + LN_EPS) * lng_ref[...] + lnb_ref[...]
    xo_ref[...] = xn
    if emit_next:
        xm_ref[...] = (xn * (1.0 + scn_ref[pl.ds(r, 1), :]) + shn_ref[pl.ds(r, 1), :]).astype(xm_ref.dtype)


def resid_ln(y, x, mod, ln_g, ln_b, layer, sub, dims, next_mod=None, next_dtype=BF16):
    n, d = x.shape
    tm = 256
    row = pl.BlockSpec((tm, d), lambda i: (i, 0))
    vec = pl.BlockSpec((None, 1, d), lambda i: (2 * layer + sub, 0, 0))
    in_specs = [row, row, _mod_spec(layer, 2 + 3 * sub, 1), vec, vec]
    args = [y, x, mod, ln_g, ln_b]
    out_shape = [jax.ShapeDtypeStruct((n, d), F32)]
    out_specs = [row]
    if next_mod is not None:
        nl, nsc, nsh = next_mod
        in_specs += [_mod_spec(nl, nsc, 1), _mod_spec(nl, nsh, 1)]
        args += [mod, mod]
        out_shape.append(jax.ShapeDtypeStruct((n, d), next_dtype))
        out_specs.append(row)
    res = pl.pallas_call(
        functools.partial(_resid_ln_kernel, dims=dims, tm=tm, emit_next=next_mod is not None),
        grid=(n // tm,),
        in_specs=in_specs,
        out_specs=out_specs,
        out_shape=out_shape,
        compiler_params=_params(("parallel",)),
        name="resid_ln",
    )(*args)
    return (res[0], res[1]) if next_mod is not None else (res[0], None)


def _lb_kernel(l_ref, o_ref):
    x = l_ref[...]
    n_even = x.shape[1]
    m = jnp.max(x, axis=1, keepdims=True)
    e = jnp.exp(x - m)
    p = e / jnp.sum(e, axis=1, keepdims=True)
    run = jnp.zeros_like(p[:, 0:1, :])
    for j in range(n_even):
        o_ref[:, j:j + 1, :] = run
        if j + 1 < n_even:
            run = run + p[:, j + 1:j + 2, :]


def hgrn_lower_bounds(lb_logits):
    return pl.pallas_call(
        _lb_kernel,
        out_shape=jax.ShapeDtypeStruct(lb_logits.shape, F32),
        name="hgrn_lb",
    )(lb_logits)


def _hgrn_prep(raw_ref, lb_row, bc_ref, k_ref, tb, upper):
    lbv = lb_row
    pos = lbv > 0.0
    loglb = jnp.where(pos, jnp.log(jnp.where(pos, lbv, 1.0)), NEG_BIG)
    l1m = jnp.log1p(-lbv)
    ri = lax.broadcasted_iota(jnp.int32, (128, 128), 0)
    ci = lax.broadcasted_iota(jnp.int32, (128, 128), 1)
    same = (ri // SCAN_CHUNK) == (ci // SCAN_CHUNK)
    tri = (ci >= ri) if upper else (ci <= ri)
    lmat = jnp.where(same, jnp.where(tri, 1.0, 0.0), 0.0).astype(F32)
    for rb in range(tb // 128):
        rows = pl.ds(rb * 128, 128)
        raw = raw_ref[rows, :]
        ls = jnp.minimum(raw, 0.0) - jnp.log1p(jnp.exp(-jnp.abs(raw)))
        b = l1m + ls
        logf = jnp.maximum(loglb, b) + jnp.log1p(jnp.exp(-jnp.abs(loglb - b)))
        k_ref[rows, :] = (1.0 - lbv) / (1.0 + jnp.exp(raw))
        bc_ref[rows, :] = jnp.dot(lmat, logf, precision=HIGHEST, preferred_element_type=F32)


def _hgrn_chunk(r0, q_ref, v_ref, bc_ref, k_ref, s_ref, o_ref, backward):
    c = SCAN_CHUNK
    rows = pl.ds(r0, c)
    qs = _silu(q_ref[rows, :])
    v = v_ref[rows, :]
    bc = bc_ref[rows, :]
    k = k_ref[rows, :]
    t_idx = lax.broadcasted_iota(jnp.int32, (c, 1), 0)
    acc = [jnp.zeros((c, DK_A), F32) for _ in range(H_A)]
    for s in range(c):
        bcs = bc_ref[pl.ds(r0 + s, 1), :]
        ks = k_ref[pl.ds(r0 + s, 1), :]
        vs = v_ref[pl.ds(r0 + s, 1), :]
        valid = (t_idx <= s) if backward else (t_idx >= s)
        e = jnp.where(valid, jnp.exp(jnp.minimum(bc - bcs, 0.0)), 0.0)
        a_full = qs * ks * e
        for h in range(H_A):
            sl = slice(h * DK_A, (h + 1) * DK_A)
            a = jnp.sum(a_full[:, sl], axis=-1, keepdims=True)
            acc[h] = acc[h] + a * vs[:, sl]
    bend = bc_ref[pl.ds(r0, 1), :] if backward else bc_ref[pl.ds(r0 + c - 1, 1), :]
    qd = (qs * jnp.exp(bc)).astype(BF16)
    kd = (k * jnp.exp(bend - bc)).astype(BF16)
    dec = jnp.exp(bend)
    vb = v.astype(BF16)
    for h in range(H_A):
        sl = slice(h * DK_A, (h + 1) * DK_A)
        st = s_ref[h]
        inter = lax.dot_general(qd[:, sl], st.astype(BF16), (((1,), (1,)), ((), ())), preferred_element_type=F32)
        upd = lax.dot_general(vb[:, sl], kd[:, sl], (((0,), (0,)), ((), ())), preferred_element_type=F32)
        s_ref[h] = st * dec[:, sl] + upd
        o_ref[rows, sl] = acc[h] + inter


def _hgrn_kernel(*refs, tb, has_init, emit_state):
    it = iter(refs)
    qf, ff, vf, qb, fb, vb, lb = [next(it) for _ in range(7)]
    if has_init:
        s0f, s0b = next(it), next(it)
    of, ob = next(it), next(it)
    if emit_state:
        sfo, sbo = next(it), next(it)
    sf, sb, bcf, kf, bcb, kb = [next(it) for _ in range(6)]
    j = pl.program_id(1)

    @pl.when(j == 0)
    def _():
        if has_init:
            sf[...] = s0f[...]
            sb[...] = s0b[...]
        else:
            sf[...] = jnp.zeros_like(sf)
            sb[...] = jnp.zeros_like(sb)

    _hgrn_prep(ff, lb[0:1, :], bcf, kf, tb, upper=False)
    _hgrn_prep(fb, lb[1:2, :], bcb, kb, tb, upper=True)
    nchunks = tb // SCAN_CHUNK

    def body(ci, carry):
        _hgrn_chunk(pl.multiple_of(ci * SCAN_CHUNK, SCAN_CHUNK), qf, vf, bcf, kf, sf, of, backward=False)
        _hgrn_chunk(pl.multiple_of((nchunks - 1 - ci) * SCAN_CHUNK, SCAN_CHUNK), qb, vb, bcb, kb, sb, ob,
                    backward=True)
        return carry

    lax.fori_loop(0, nchunks, body, 0)

    if emit_state:
        @pl.when(j == pl.num_programs(1) - 1)
        def _():
            sfo[...] = sf[...]
            sbo[...] = sb[...]


def hgrn_scan(h_in, lb, row0, bsz, t, init=None, emit_state=False):
    tb = min(SCAN_BLOCK, t)
    assert t % tb == 0 and row0 % tb == 0 and tb % 128 == 0
    nt = t // tb
    base = row0 // tb
    fwd = lambda c: pl.BlockSpec((tb, A_W), lambda b, j: (base + b * nt + j, c))
    bwd = lambda c: pl.BlockSpec((tb, A_W), lambda b, j: (base + b * nt + nt - 1 - j, c))
    state = pl.BlockSpec((None, H_A, DK_A, DK_A), lambda b, j: (b, 0, 0, 0))
    in_specs = [fwd(0), fwd(1), fwd(3), bwd(0), bwd(2), bwd(3), pl.BlockSpec((2, A_W), lambda b, j: (0, 0))]
    args = [h_in] * 6 + [lb]
    if init is not None:
        in_specs += [state, state]
        args += list(init)
    out_specs = [pl.BlockSpec((tb, A_W), lambda b, j: (b * nt + j, 0)),
                 pl.BlockSpec((tb, A_W), lambda b, j: (b * nt + nt - 1 - j, 0))]
    out_shape = [jax.ShapeDtypeStruct((bsz * t, A_W), F32)] * 2
    if emit_state:
        out_specs += [state, state]
        out_shape += [jax.ShapeDtypeStruct((bsz, H_A, DK_A, DK_A), F32)] * 2
    scratch = [pltpu.VMEM((H_A, DK_A, DK_A), F32)] * 2 + [pltpu.VMEM((tb, A_W), F32)] * 4
    return pl.pallas_call(
        functools.partial(_hgrn_kernel, tb=tb, has_init=init is not None, emit_state=emit_state),
        grid=(bsz, nt),
        in_specs=in_specs,
        out_specs=out_specs,
        out_shape=out_shape,
        scratch_shapes=scratch,
        compiler_params=_params(("parallel", "arbitrary")),
        name="hgrn_scan",
    )(*args)


def _even_post_kernel(of_ref, ob_ref, g_ref, u_ref, v_ref, gn_ref, ws_ref, bs_ref, lg_ref, lb_ref, o_ref):
    o = of_ref[...] + ob_ref[...]
    gate = _silu(g_ref[...])
    gn = gn_ref[...]
    for h in range(H_A):
        sl = slice(h * DK_A, (h + 1) * DK_A)
        oh = o[:, sl]
        r = lax.rsqrt(jnp.mean(oh * oh, axis=-1, keepdims=True) + RMS_EPS)
        o_ref[:, sl] = (oh * r * gn * gate[:, sl]).astype(o_ref.dtype)
    u = _gelu_tanh(u_ref[...])
    v = _gelu_tanh(v_ref[...])
    mu = jnp.mean(v, axis=-1, keepdims=True)
    vc = v - mu
    var = jnp.mean(vc * vc, axis=-1, keepdims=True)
    vn = (vc * lax.rsqrt(var + LN_EPS) * lg_ref[...] + lb_ref[...]).astype(BF16)
    gd = B_W // B_GROUPS
    for g in range(B_GROUPS):
        sl = slice(g * gd, (g + 1) * gd)
        mixed = jnp.dot(ws_ref[g].astype(BF16), vn[:, sl], preferred_element_type=F32) + bs_ref[:, sl]
        o_ref[:, A_W + g * gd:A_W + (g + 1) * gd] = (u[:, sl] * mixed).astype(o_ref.dtype)


def even_post(o_fw, o_bw, h_in, gnorm, sgu_w, sgu_b_full, sgu_ln_g, sgu_ln_b):
    n = h_in.shape[0]
    tm = CHUNK_B
    col = lambda c: pl.BlockSpec((tm, A_W), lambda i: (i, c))
    full2 = lambda a: pl.BlockSpec(a.shape, lambda i: (0, 0))
    return pl.pallas_call(
        _even_post_kernel,
        grid=(n // tm,),
        in_specs=[col(0), col(0), col(4), col(5), col(6), full2(gnorm),
                  pl.BlockSpec(sgu_w.shape, lambda i: (0, 0, 0)), full2(sgu_b_full), full2(sgu_ln_g),
                  full2(sgu_ln_b)],
        out_specs=pl.BlockSpec((tm, A_W + B_W), lambda i: (i, 0)),
        out_shape=jax.ShapeDtypeStruct((n, A_W + B_W), BF16),
        compiler_params=_params(("parallel",)),
        name="even_post",
    )(o_fw, o_bw, h_in, h_in, h_in, gnorm, sgu_w, sgu_b_full, sgu_ln_g, sgu_ln_b)


def _conv_kernel(bg_ref, cg_ref, hc_ref, cgp_ref, hcp_ref, cgn_ref, hcn_ref, w_ref, o_ref, *, dims, tm):
    i = pl.program_id(0)
    npt = dims.n_prompt // tm
    pos = jnp.where(i < npt, i % (dims.tp // tm), (i - npt) % (dims.ts // tm))
    last = jnp.where(i < npt, dims.tp // tm - 1, dims.ts // tm - 1)
    has_prev = (pos != 0).astype(F32)
    has_next = (pos != last).astype(F32)
    z = cg_ref[...] * hc_ref[...]
    z_before = cgp_ref[7:8, :] * hcp_ref[7:8, :] * has_prev
    z_after = cgn_ref[0:1, :] * hcn_ref[0:1, :] * has_next
    row = lax.broadcasted_iota(jnp.int32, (tm, 1), 0)
    zp = jnp.where(row == 0, z_before, pltpu.roll(z, 1, 0))
    zn = jnp.where(row == tm - 1, z_after, pltpu.roll(z, tm - 1, 0))
    y = w_ref[0:1, :] * zp + w_ref[1:2, :] * z + w_ref[2:3, :] * zn
    o_ref[...] = (bg_ref[...] * y).astype(o_ref.dtype)


def short_conv(h_in, conv_w, dims):
    n = h_in.shape[0]
    tm = ROW_TILE
    sub = tm // 8
    nblk8 = n // 8
    cur = lambda c: pl.BlockSpec((tm, C_W), lambda i: (i, c))
    prev = lambda c: pl.BlockSpec((8, C_W), lambda i: (jnp.maximum(i * sub - 1, 0), c))
    nxt = lambda c: pl.BlockSpec((8, C_W), lambda i: (jnp.minimum((i + 1) * sub, nblk8 - 1), c))
    return pl.pallas_call(
        functools.partial(_conv_kernel, dims=dims, tm=tm),
        grid=(n // tm,),
        in_specs=[cur(0), cur(1), cur(2), prev(1), prev(2), nxt(1), nxt(2),
                  pl.BlockSpec(conv_w.shape, lambda i: (0, 0))],
        out_specs=pl.BlockSpec((tm, C_W), lambda i: (i, 0)),
        out_shape=jax.ShapeDtypeStruct((n, C_W), BF16),
        compiler_params=_params(("parallel",)),
        name="short_conv",
    )(h_in, h_in, h_in, h_in, h_in, h_in, h_in, conv_w)


def _ctx_attn_kernel(q_ref, k_ref, v_ref, o_ref, kc_ref, vc_ref):
    scale = HD_D ** -0.5
    for h in range(H_D):
        sl = slice(h * HD_D, (h + 1) * HD_D)
        k = k_ref[:, sl]
        v = v_ref[:, sl]
        kc_ref[h] = k
        vc_ref[h] = v
        s = lax.dot_general(q_ref[:, sl].astype(BF16), k.astype(BF16), (((1,), (1,)), ((), ())),
                            preferred_element_type=F32) * scale
        m = jnp.max(s, axis=-1, keepdims=True)
        e = jnp.exp(s - m)
        p = (e / jnp.sum(e, axis=-1, keepdims=True)).astype(BF16)
        o_ref[:, sl] = jnp.dot(p, v.astype(BF16), preferred_element_type=F32).astype(o_ref.dtype)


def context_attention(h_in, bsz, t):
    blk = lambda c: pl.BlockSpec((t, D_W), lambda b: (b, c))
    cache = pl.BlockSpec((None, H_D, t, HD_D), lambda b: (b, 0, 0, 0))
    return pl.pallas_call(
        _ctx_attn_kernel,
        grid=(bsz,),
        in_specs=[blk(3), blk(4), blk(5)],
        out_specs=[pl.BlockSpec((t, D_W), lambda b: (b, 0)), cache, cache],
        out_shape=[jax.ShapeDtypeStruct((bsz * t, D_W), BF16),
                   jax.ShapeDtypeStruct((bsz, H_D, t, HD_D), F32),
                   jax.ShapeDtypeStruct((bsz, H_D, t, HD_D), F32)],
        compiler_params=_params(("parallel",)),
        name="ctx_attn",
    )(h_in, h_in, h_in)


def _na_bias_kernel(r_ref, oh_ref, o_ref):
    o_ref[...] = jnp.dot(r_ref[...], oh_ref[...], precision=HIGHEST, preferred_element_type=F32)


def na_bias_table(rpb, rows):
    nh, ndr, ndc = rpb.shape
    wr = min(WIN_R, rows)
    col = np.arange(GRID_W)
    dc = np.clip(col[None, :] - col[:, None] + WIN_C - 1, 0, 2 * WIN_C - 2)
    onehot = (dc.reshape(1, -1) == np.arange(32).reshape(-1, 1)).astype(np.float32)
    m = nh * ndr
    mp = -(-m // 8) * 8
    r2 = jnp.zeros((mp, 32), F32).at[:m, :ndc].set(rpb.reshape(m, ndc))
    b = pl.pallas_call(
        _na_bias_kernel,
        out_shape=jax.ShapeDtypeStruct((mp, GRID_W * GRID_W), F32),
        name="na_bias",
    )(r2, jnp.asarray(onehot))
    b = b[:m].reshape(nh, ndr, GRID_W, GRID_W)
    cs = np.clip(col - WIN_C // 2, 0, GRID_W - WIN_C)
    col_ok = (col[None, :] >= cs[:, None]) & (col[None, :] < cs[:, None] + WIN_C)
    b = jnp.where(jnp.asarray(col_ok)[None, None], b, NEG_BIG)
    offs = []
    for off in range(WIN_R):
        idx = np.clip(off + np.arange(wr), 0, ndr - 1)
        offs.append(jnp.transpose(b[:, idx], (0, 2, 1, 3)).reshape(nh, GRID_W, wr * GRID_W))
    return jnp.stack(offs, axis=1)


def _na_attn_kernel(q_ref, k_ref, v_ref, kc_ref, vc_ref, bias_ref, o_ref, kb, vb, kcb, vcb, *, rows, wr):
    scale = HD_D ** -0.5
    kb[...] = k_ref[...].astype(BF16)
    vb[...] = v_ref[...].astype(BF16)
    kcb[...] = kc_ref[...].astype(BF16)
    vcb[...] = vc_ref[...].astype(BF16)
    w = GRID_W

    def body(r, carry):
        rs = jnp.clip(r - wr // 2, 0, rows - wr)
        off = rs - r + WIN_R - 1
        q = q_ref[pl.ds(pl.multiple_of(r * w, w), w), :].astype(BF16)
        win = pl.ds(pl.multiple_of(rs * w, w), wr * w)
        s_loc = lax.dot_general(q, kb[win, :], (((1,), (1,)), ((), ())), preferred_element_type=F32) * scale
        s_loc = s_loc + bias_ref[off]
        s_ctx = lax.dot_general(q, kcb[...], (((1,), (1,)), ((), ())), preferred_element_type=F32) * scale
        m = jnp.maximum(jnp.max(s_loc, axis=-1, keepdims=True), jnp.max(s_ctx, axis=-1, keepdims=True))
        e_loc = jnp.exp(s_loc - m)
        e_ctx = jnp.exp(s_ctx - m)
        inv = 1.0 / (jnp.sum(e_loc, axis=-1, keepdims=True) + jnp.sum(e_ctx, axis=-1, keepdims=True))
        o = (jnp.dot((e_loc * inv).astype(BF16), vb[win, :], preferred_element_type=F32)
             + jnp.dot((e_ctx * inv).astype(BF16), vcb[...], preferred_element_type=F32))
        o_ref[pl.ds(pl.multiple_of(r * w, w), w), :] = o.astype(o_ref.dtype)
        return carry

    lax.fori_loop(0, rows, body, 0)


def neighbourhood_attention(h_in, cache_k, cache_v, bias, row0, bsz, t):
    assert row0 % t == 0 and t % GRID_W == 0
    rows = t // GRID_W
    wr = min(WIN_R, rows)
    base = row0 // t
    past = cache_k.shape[2]
    qkv = lambda c: pl.BlockSpec((t, HD_D), lambda b, h: (base + b, c * H_D + h))
    ctx = pl.BlockSpec((None, None, past, HD_D), lambda b, h: (b, h, 0, 0))
    return pl.pallas_call(
        functools.partial(_na_attn_kernel, rows=rows, wr=wr),
        grid=(bsz, H_D),
        in_specs=[qkv(3), qkv(4), qkv(5), ctx, ctx,
                  pl.BlockSpec((None, WIN_R, GRID_W, wr * GRID_W), lambda b, h: (h, 0, 0, 0))],
        out_specs=pl.BlockSpec((t, HD_D), lambda b, h: (b, h)),
        out_shape=jax.ShapeDtypeStruct((bsz * t, D_W), BF16),
        scratch_shapes=[pltpu.VMEM((t, HD_D), BF16), pltpu.VMEM((t, HD_D), BF16),
                        pltpu.VMEM((past, HD_D), BF16), pltpu.VMEM((past, HD_D), BF16)],
        compiler_params=_params(("parallel", "parallel")),
        name="na_attn",
    )(h_in, h_in, h_in, cache_k, cache_v, bias)


def _router_kernel(x_ref, w_ref, idx_ref, wt_ref):
    logits = jnp.dot(x_ref[...], w_ref[...], precision=HIGHEST, preferred_element_type=F32)
    lane = lax.broadcasted_iota(jnp.int32, logits.shape, 1)
    logits = jnp.where(lane < N_EXPERTS, logits, -jnp.inf)
    m1 = jnp.max(logits, axis=-1, keepdims=True)
    i1 = jnp.min(jnp.where(logits == m1, lane, 128), axis=-1, keepdims=True)
    rest = jnp.where(lane == i1, -jnp.inf, logits)
    m2 = jnp.max(rest, axis=-1, keepdims=True)
    i2 = jnp.min(jnp.where(rest == m2, lane, 128), axis=-1, keepdims=True)
    e2 = jnp.exp(m2 - m1)
    w1 = 1.0 / (1.0 + e2)
    w2 = e2 / (1.0 + e2)
    idx_ref[...] = jnp.where(lane == 0, i1, jnp.where(lane == 1, i2, 0))
    wt_ref[...] = jnp.where(lane == 0, w1, jnp.where(lane == 1, w2, 0.0))


def moe_router(xm, router):
    n, d = xm.shape
    tm = 512
    wpad = jnp.zeros((d, 128), F32).at[:, :N_EXPERTS].set(router)
    idx, wt = pl.pallas_call(
        _router_kernel,
        grid=(n // tm,),
        in_specs=[pl.BlockSpec((tm, d), lambda i: (i, 0)), pl.BlockSpec((d, 128), lambda i: (0, 0))],
        out_specs=[pl.BlockSpec((tm, 128), lambda i: (i, 0))] * 2,
        out_shape=[jax.ShapeDtypeStruct((n, 128), jnp.int32), jax.ShapeDtypeStruct((n, 128), F32)],
        compiler_params=_params(("parallel",)),
        name="router",
    )(xm, wpad)
    return idx[:, :2], wt[:, :2]


GATHER_ROWS = 256


def _gather_kernel(src_ref, x_hbm, o_ref, buf, sem):
    nrows = buf.shape[0]

    def copy(r):
        return pltpu.make_async_copy(x_hbm.at[pl.ds(src_ref[0, r], 1), :], buf.at[pl.ds(r, 1), :], sem)

    def start(r, c):
        copy(r).start()
        return c

    def wait(r, c):
        copy(r).wait()
        return c

    lax.fori_loop(0, nrows, start, 0)
    lax.fori_loop(0, nrows, wait, 0)
    o_ref[...] = buf[...].astype(o_ref.dtype)


def gather_rows(x, src, out_dtype):
    p = src.shape[0]
    d = x.shape[1]
    r = GATHER_ROWS
    return pl.pallas_call(
        _gather_kernel,
        grid=(p // r,),
        in_specs=[pl.BlockSpec((None, 1, r), lambda i: (i, 0, 0), memory_space=pltpu.SMEM),
                  pl.BlockSpec(memory_space=pl.ANY)],
        out_specs=pl.BlockSpec((r, d), lambda i: (i, 0)),
        out_shape=jax.ShapeDtypeStruct((p, d), out_dtype),
        scratch_shapes=[pltpu.VMEM((r, d), F32), pltpu.SemaphoreType.DMA(())],
        compiler_params=_params(("arbitrary",)),
        name="gather_rows",
    )(src.reshape(p // r, 1, r), x)


def _combine_kernel(p1_ref, p2_ref, w_ref, y_hbm, o_ref, buf1, buf2, sem1, sem2):
    nrows = buf1.shape[0]

    def copy1(r):
        return pltpu.make_async_copy(y_hbm.at[pl.ds(p1_ref[0, r], 1), :], buf1.at[pl.ds(r, 1), :], sem1)

    def copy2(r):
        return pltpu.make_async_copy(y_hbm.at[pl.ds(p2_ref[0, r], 1), :], buf2.at[pl.ds(r, 1), :], sem2)

    def start(r, c):
        copy1(r).start()
        copy2(r).start()
        return c

    def wait(r, c):
        copy1(r).wait()
        copy2(r).wait()
        return c

    lax.fori_loop(0, nrows, start, 0)
    lax.fori_loop(0, nrows, wait, 0)
    w = w_ref[...]
    o_ref[...] = w[:, 0:1] * buf1[...] + w[:, 1:2] * buf2[...]


def moe_combine(y, pos1, pos2, wt):
    n = pos1.shape[0]
    d = y.shape[1]
    r = GATHER_ROWS
    wpad = jnp.zeros((n, 128), F32).at[:, :2].set(wt)
    idx = pl.BlockSpec((None, 1, r), lambda i: (i, 0, 0), memory_space=pltpu.SMEM)
    return pl.pallas_call(
        _combine_kernel,
        grid=(n // r,),
        in_specs=[idx, idx, pl.BlockSpec((r, 128), lambda i: (i, 0)), pl.BlockSpec(memory_space=pl.ANY)],
        out_specs=pl.BlockSpec((r, d), lambda i: (i, 0)),
        out_shape=jax.ShapeDtypeStruct((n, d), F32),
        scratch_shapes=[pltpu.VMEM((r, d), F32), pltpu.VMEM((r, d), F32),
                        pltpu.SemaphoreType.DMA(()), pltpu.SemaphoreType.DMA(())],
        compiler_params=_params(("arbitrary",)),
        name="moe_combine",
    )(pos1.reshape(n // r, 1, r), pos2.reshape(n // r, 1, r), wpad, y)


MOE_TILE = 512


def moe_routing(idx, tm):
    n = idx.shape[0]
    p = 2 * n + N_EXPERTS * tm
    e_flat = idx.reshape(-1)
    onehot = (e_flat[:, None] == jnp.arange(N_EXPERTS, dtype=jnp.int32)[None, :]).astype(jnp.int32)
    csum = jnp.cumsum(onehot, axis=0)
    counts = csum[-1]
    rank = jnp.sum(onehot * (csum - 1), axis=1)
    padded = ((counts + tm - 1) // tm) * tm
    ends = jnp.cumsum(padded)
    starts = ends - padded
    pos = starts[e_flat] + rank
    src = jnp.zeros((p,), jnp.int32).at[pos].set(jnp.arange(2 * n, dtype=jnp.int32) // 2)
    tile0 = jnp.arange(p // tm, dtype=jnp.int32) * tm
    tile_expert = jnp.minimum(jnp.sum((tile0[:, None] >= ends[None, :]).astype(jnp.int32), axis=1),
                              N_EXPERTS - 1).astype(jnp.int32)
    tile_valid = (tile0 < ends[-1]).astype(jnp.int32)
    pos2 = pos.reshape(n, 2)
    return src, pos2[:, 0], pos2[:, 1], tile_expert, tile_valid


def _even_layer_mixer(xm, j, dims, states, w_in, w_out, lb, gnorm, sgu_w, sgu_b, sgu_ln_g, sgu_ln_b):
    h_in = matmul(xm, w_in)
    lbj = lb[:, j, :]
    t_last = lambda s: jnp.swapaxes(s, -1, -2)
    ofp, obp, sfp, sbp = hgrn_scan(h_in, lbj, 0, dims.bp, dims.tp, init=None, emit_state=True)
    ofs, obs = hgrn_scan(h_in, lbj, dims.n_prompt, dims.bs, dims.ts,
                         init=(t_last(states[0]), t_last(states[1])), emit_state=False)
    o_fw = jnp.concatenate([ofp, ofs], axis=0)
    o_bw = jnp.concatenate([obp, obs], axis=0)
    sgu_b_full = jnp.repeat(sgu_b.T, B_W // B_GROUPS, axis=1)
    mix_in = even_post(o_fw, o_bw, h_in, gnorm.reshape(1, -1), sgu_w, sgu_b_full,
                       sgu_ln_g.reshape(1, -1), sgu_ln_b.reshape(1, -1))
    return matmul(mix_in, w_out), (t_last(sfp), t_last(sbp))


def _odd_layer_mixer(xm, dims, states, w_in, w_out, conv_w, rpb):
    h_in = matmul(xm, w_in)
    y_c = short_conv(h_in, conv_w, dims)
    o_p, k_new, v_new = context_attention(h_in, dims.bp, dims.tp)
    bias = na_bias_table(rpb, dims.ts // GRID_W)
    o_s = neighbourhood_attention(h_in, states[0], states[1], bias, dims.n_prompt, dims.bs, dims.ts)
    mix_in = jnp.concatenate([y_c, jnp.concatenate([o_p, o_s], axis=0)], axis=1)
    return matmul(mix_in, w_out), (k_new, v_new)


def _dense_ffn(xm, w_gu, w_down):
    n = xm.shape[0]
    tm = 1024 if n % 1024 == 0 else 512
    te = jnp.zeros((n // tm,), jnp.int32)
    tv = jnp.ones((n // tm,), jnp.int32)
    h = grouped_gate_up(xm, w_gu[None], te, tv, tm)
    tm2 = 512
    te2 = jnp.zeros((n // tm2,), jnp.int32)
    tv2 = jnp.ones((n // tm2,), jnp.int32)
    return grouped_down(h, w_down[None], te2, tv2, tm2)


def _moe_ffn(xm, router, w_gu, w_down):
    idx, wt = moe_router(xm, router)
    tm = MOE_TILE
    src, pos1, pos2, tile_expert, tile_valid = moe_routing(idx, tm)
    xs = gather_rows(xm, src, BF16)
    h = grouped_gate_up(xs, w_gu, tile_expert, tile_valid, tm)
    y = grouped_down(h, w_down, tile_expert, tile_valid, tm)
    return moe_combine(y, pos1, pos2, wt)


def trunk(dims, x, cond, cached, w_mod, b_mod, ln_g, ln_b, w_in_even, w_out_even, hgrn_lb_logits, hgrn_gnorm,
          sgu_w, sgu_b, sgu_ln_g, sgu_ln_b, ffn_w_gu, ffn_w_down, w_in_odd, w_out_odd, conv_w, na_rpb,
          moe_router_w, moe_w_gu, moe_w_down):
    depth = w_mod.shape[0]
    bf = lambda a: a.astype(BF16)
    mod = modulation_table(cond, w_mod, b_mod)
    lb = hgrn_lower_bounds(hgrn_lb_logits)
    lng = ln_g.reshape(depth * 2, 1, -1)
    lnb = ln_b.reshape(depth * 2, 1, -1)
    xm = premodulate(x, mod, 0, dims)
    new_state = []
    for l in range(depth):
        j = l // 2
        if l % 2 == 0:
            mix, st = _even_layer_mixer(xm, j, dims, cached[l], bf(w_in_even[j]), bf(w_out_even[j]), lb,
                                        hgrn_gnorm[j], sgu_w[j], sgu_b[j], sgu_ln_g[j], sgu_ln_b[j])
        else:
            mix, st = _odd_layer_mixer(xm, dims, cached[l], bf(w_in_odd[j]), bf(w_out_odd[j]), conv_w[j], na_rpb[j])
        new_state.extend(st)
        moe = l % 2 == 1
        x, xm = resid_ln(mix, x, mod, lng, lnb, l, 0, dims, next_mod=(l, 4, 3), next_dtype=F32 if moe else BF16)
        if moe:
            y = _moe_ffn(xm, moe_router_w[j], bf(moe_w_gu[j]), bf(moe_w_down[j]))
        else:
            y = _dense_ffn(xm, bf(ffn_w_gu[j]), bf(ffn_w_down[j]))
        nxt = (l + 1, 1, 0) if l + 1 < depth else None
        x, xm = resid_ln(y, x, mod, lng, lnb, l, 1, dims, next_mod=nxt)
    return x, new_state


def kernel(x_prompt, x_sample, state_fwd_0, state_bwd_0, cache_k_1, cache_v_1, state_fwd_2, state_bwd_2, cache_k_3, cache_v_3, c, c_ctx, w_mod, b_mod, ln_g, ln_b, w_in_even, w_out_even, hgrn_lb_logits, hgrn_gnorm, sgu_w, sgu_b, sgu_ln_g, sgu_ln_b, ffn_w_gu, ffn_w_down, w_in_odd, w_out_odd, conv_w, na_rpb, moe_router, moe_w_gu, moe_w_down):
    bp, tp, d = x_prompt.shape
    bs, ts, _ = x_sample.shape
    dims = Dims(bp, tp, bs, ts)
    assert bs + 1 <= MOD_ROWS
    x = jnp.concatenate([x_prompt.reshape(bp * tp, d), x_sample.reshape(bs * ts, d)], axis=0)
    cond = jnp.zeros((MOD_ROWS, d), F32).at[0].set(c_ctx).at[1:1 + bs].set(c)
    cached = [(state_fwd_0, state_bwd_0), (cache_k_1, cache_v_1), (state_fwd_2, state_bwd_2), (cache_k_3, cache_v_3)]
    x, new_state = trunk(dims, x, cond, cached, w_mod, b_mod, ln_g, ln_b, w_in_even, w_out_even, hgrn_lb_logits,
                         hgrn_gnorm, sgu_w, sgu_b, sgu_ln_g, sgu_ln_b, ffn_w_gu, ffn_w_down, w_in_odd, w_out_odd,
                         conv_w, na_rpb, moe_router, moe_w_gu, moe_w_down)
    y_prompt = x[:bp * tp].reshape(bp, tp, d)
    y_sample = x[bp * tp:].reshape(bs, ts, d)
    return (y_prompt, y_sample, *new_state)
```

```python
import functools
from typing import NamedTuple

import numpy as np
import jax
import jax.numpy as jnp
from jax import lax
from jax.experimental import pallas as pl
from jax.experimental.pallas import tpu as pltpu

F32 = jnp.float32
BF16 = jnp.bfloat16
HIGHEST = lax.Precision.HIGHEST

D_MODEL = 2048
DEPTH = 4
GRID_W = 64
H_A = 8
DK_A = 128
A_W = H_A * DK_A
B_GROUPS = 8
B_W = 1024
CHUNK_B = 128
C_W = 1024
H_D = 8
HD_D = 128
D_W = H_D * HD_D
WIN_R = 8
WIN_C = 16
D_FF = 7168
N_EXPERTS = 8
ALPHA = (2 * DEPTH) ** 0.25
LN_EPS = 1e-5
RMS_EPS = 1e-6

MOD_ROWS = 16
SCAN_CHUNK = 16
SCAN_BLOCK = 256
SAFE_DECAY = 60.0
ROW_TILE = 256
NEG_BIG = -1e30
VMEM_LIMIT = 56 * 1024 * 1024


class Dims(NamedTuple):
    bp: int
    tp: int
    bs: int
    ts: int

    @property
    def n_prompt(self):
        return self.bp * self.tp

    @property
    def n_sample(self):
        return self.bs * self.ts

    @property
    def n(self):
        return self.bp * self.tp + self.bs * self.ts


def _params(sem, vmem=VMEM_LIMIT):
    return pltpu.CompilerParams(dimension_semantics=sem, vmem_limit_bytes=vmem)


def _sigmoid(x):
    return 1.0 / (1.0 + jnp.exp(-x))


def _silu(x):
    return x * _sigmoid(x)


def _gelu_tanh(x):
    return 0.5 * x * (1.0 + jnp.tanh(np.sqrt(2.0 / np.pi).astype(np.float32) * (x + 0.044715 * (x * x * x))))


def _mod_row(tile, tm, dims):
    r0 = tile * tm
    return jnp.where(r0 < dims.n_prompt, 0, 1 + (r0 - dims.n_prompt) // dims.ts)


def _mod_spec(layer, which, nargs):
    if nargs == 1:
        return pl.BlockSpec((None, None, MOD_ROWS, D_MODEL), lambda i: (layer, which, 0, 0))
    return pl.BlockSpec((None, None, MOD_ROWS, D_MODEL), lambda i, j: (layer, which, 0, 0))


def _mod_kernel(cond_ref, w_ref, b_ref, o_ref):
    c = cond_ref[...]
    s = _silu(c).astype(BF16)
    o_ref[...] = jnp.dot(s, w_ref[...].astype(BF16), preferred_element_type=F32) + b_ref[...]


def modulation_table(cond, w_mod, b_mod):
    depth, d, _ = w_mod.shape
    tn = 1024
    nb = d // tn
    return pl.pallas_call(
        _mod_kernel,
        grid=(depth, 6, nb),
        in_specs=[pl.BlockSpec((MOD_ROWS, d), lambda l, w, n: (0, 0)),
                  pl.BlockSpec((None, d, tn), lambda l, w, n: (l, 0, w * nb + n)),
                  pl.BlockSpec((None, 1, tn), lambda l, w, n: (l, 0, w * nb + n))],
        out_specs=pl.BlockSpec((None, None, MOD_ROWS, tn), lambda l, w, n: (l, w, 0, n)),
        out_shape=jax.ShapeDtypeStruct((depth, 6, MOD_ROWS, d), F32),
        compiler_params=_params(("parallel", "parallel", "parallel")),
        name="modulation",
    )(cond, w_mod, b_mod.reshape(depth, 1, 6 * d))


def _premod_kernel(x_ref, sc_ref, sh_ref, o_ref, *, dims, tm):
    r = _mod_row(pl.program_id(0), tm, dims)
    o_ref[...] = (x_ref[...] * (1.0 + sc_ref[pl.ds(r, 1), :]) + sh_ref[pl.ds(r, 1), :]).astype(o_ref.dtype)


def premodulate(x, mod, layer, dims):
    n, d = x.shape
    tm = 256
    return pl.pallas_call(
        functools.partial(_premod_kernel, dims=dims, tm=tm),
        grid=(n // tm,),
        in_specs=[pl.BlockSpec((tm, d), lambda i: (i, 0)), _mod_spec(layer, 1, 1), _mod_spec(layer, 0, 1)],
        out_specs=pl.BlockSpec((tm, d), lambda i: (i, 0)),
        out_shape=jax.ShapeDtypeStruct((n, d), BF16),
        compiler_params=_params(("parallel",)),
        name="premod",
    )(x, mod, mod)


def _mm_kernel(x_ref, w_ref, o_ref, wb_ref):
    @pl.when(pl.program_id(1) == 0)
    def _():
        wb_ref[...] = w_ref[...].astype(BF16)

    o_ref[...] = jnp.dot(x_ref[...], wb_ref[...], preferred_element_type=F32).astype(o_ref.dtype)


def matmul(x, w, layer, out_dtype=F32, tm=1024, tn=1024):
    n, k = x.shape
    m = w.shape[2]
    while n % tm:
        tm //= 2
    return pl.pallas_call(
        _mm_kernel,
        grid=(m // tn, n // tm),
        in_specs=[pl.BlockSpec((tm, k), lambda j, i: (i, 0)),
                  pl.BlockSpec((None, k, tn), lambda j, i: (layer, 0, j))],
        out_specs=pl.BlockSpec((tm, tn), lambda j, i: (i, j)),
        out_shape=jax.ShapeDtypeStruct((n, m), out_dtype),
        scratch_shapes=[pltpu.VMEM((k, tn), BF16)],
        compiler_params=_params(("parallel", "arbitrary")),
        name="matmul",
    )(x, w)


def _gu_kernel(te_ref, tv_ref, x_ref, wg_ref, wu_ref, o_ref, wgb_ref, wub_ref):
    i = pl.program_id(1)
    fresh = jnp.logical_or(i == 0, te_ref[i] != te_ref[jnp.maximum(i - 1, 0)])

    @pl.when(fresh)
    def _():
        wgb_ref[...] = wg_ref[...].astype(BF16)
        wub_ref[...] = wu_ref[...].astype(BF16)

    @pl.when(tv_ref[i] != 0)
    def _():
        x = x_ref[...]
        g = jnp.dot(x, wgb_ref[...], preferred_element_type=F32)
        u = jnp.dot(x, wub_ref[...], preferred_element_type=F32)
        o_ref[...] = (_silu(g) * u).astype(o_ref.dtype)

    @pl.when(tv_ref[i] == 0)
    def _():
        o_ref[...] = jnp.zeros_like(o_ref)


def grouped_gate_up(x, w_gu, layer, tile_expert, tile_valid, tm, tn=512):
    p, d = x.shape
    f = w_gu.shape[3] // 2
    nj = f // tn
    grid_spec = pltpu.PrefetchScalarGridSpec(
        num_scalar_prefetch=2,
        grid=(nj, p // tm),
        in_specs=[pl.BlockSpec((tm, d), lambda j, i, te, tv: (i, 0)),
                  pl.BlockSpec((None, None, d, tn), lambda j, i, te, tv: (layer, te[i], 0, j)),
                  pl.BlockSpec((None, None, d, tn), lambda j, i, te, tv: (layer, te[i], 0, nj + j))],
        out_specs=pl.BlockSpec((tm, tn), lambda j, i, te, tv: (i, j)),
        scratch_shapes=[pltpu.VMEM((d, tn), BF16), pltpu.VMEM((d, tn), BF16)],
    )
    return pl.pallas_call(
        _gu_kernel,
        grid_spec=grid_spec,
        out_shape=jax.ShapeDtypeStruct((p, f), BF16),
        compiler_params=_params(("parallel", "arbitrary")),
        name="gate_up",
    )(tile_expert, tile_valid, x, w_gu, w_gu)


def _down_kernel(te_ref, tv_ref, h_ref, w_ref, o_ref):
    del te_ref
    k = pl.program_id(1)
    valid = tv_ref[pl.program_id(0)] != 0

    @pl.when(k == 0)
    def _():
        o_ref[...] = jnp.zeros_like(o_ref)

    @pl.when(valid)
    def _():
        o_ref[...] += jnp.dot(h_ref[...], w_ref[...], preferred_element_type=F32)


def grouped_down(h, w_down, tile_expert, tile_valid, tm, tk=1024):
    p, f = h.shape
    d = w_down.shape[2]
    grid_spec = pltpu.PrefetchScalarGridSpec(
        num_scalar_prefetch=2,
        grid=(p // tm, f // tk),
        in_specs=[pl.BlockSpec((tm, tk), lambda i, k, te, tv: (i, k)),
                  pl.BlockSpec((None, tk, d), lambda i, k, te, tv: (te[i], k, 0))],
        out_specs=pl.BlockSpec((tm, d), lambda i, k, te, tv: (i, 0)),
    )
    return pl.pallas_call(
        _down_kernel,
        grid_spec=grid_spec,
        out_shape=jax.ShapeDtypeStruct((p, d), F32),
        compiler_params=_params(("parallel", "arbitrary")),
        name="down",
    )(tile_expert, tile_valid, h, w_down)


def _resid_ln_kernel(*refs, dims, tm, emit_next):
    if emit_next:
        y_ref, x_ref, g_ref, lng_ref, lnb_ref, scn_ref, shn_ref, xo_ref, xm_ref = refs
    else:
        y_ref, x_ref, g_ref, lng_ref, lnb_ref, xo_ref = refs
    r = _mod_row(pl.program_id(0), tm, dims)
    z = ALPHA * x_ref[...] + g_ref[pl.ds(r, 1), :] * y_ref[...]
    mu = jnp.mean(z, axis=-1, keepdims=True)
    zc = z - mu
    var = jnp.mean(zc * zc, axis=-1, keepdims=True)
    xn = zc * lax.rsqrt(var + LN_EPS) * lng_ref[...] + lnb_ref[...]
    xo_ref[...] = xn
    if emit_next:
        xm_ref[...] = (xn * (1.0 + scn_ref[pl.ds(r, 1), :]) + shn_ref[pl.ds(r, 1), :]).astype(xm_ref.dtype)


def resid_ln(y, x, mod, ln_g, ln_b, layer, sub, dims, next_mod=None, next_dtype=BF16):
    n, d = x.shape
    tm = 256
    row = pl.BlockSpec((tm, d), lambda i: (i, 0))
    vec = pl.BlockSpec((None, 1, d), lambda i: (2 * layer + sub, 0, 0))
    in_specs = [row, row, _mod_spec(layer, 2 + 3 * sub, 1), vec, vec]
    args = [y, x, mod, ln_g, ln_b]
    out_shape = [jax.ShapeDtypeStruct((n, d), F32)]
    out_specs = [row]
    if next_mod is not None:
        nl, nsc, nsh = next_mod
        in_specs += [_mod_spec(nl, nsc, 1), _mod_spec(nl, nsh, 1)]
        args += [mod, mod]
        out_shape.append(jax.ShapeDtypeStruct((n, d), next_dtype))
        out_specs.append(row)
    res = pl.pallas_call(
        functools.partial(_resid_ln_kernel, dims=dims, tm=tm, emit_next=next_mod is not None),
        grid=(n // tm,),
        in_specs=in_specs,
        out_specs=out_specs,
        out_shape=out_shape,
        compiler_params=_params(("parallel",)),
        name="resid_ln",
    )(*args)
    return (res[0], res[1]) if next_mod is not None else (res[0], None)


def _lb_kernel(l_ref, o_ref):
    x = l_ref[...]
    n_even = x.shape[1]
    m = jnp.max(x, axis=1, keepdims=True)
    e = jnp.exp(x - m)
    p = e / jnp.sum(e, axis=1, keepdims=True)
    run = jnp.zeros_like(p[:, 0:1, :])
    for j in range(n_even):
        o_ref[:, j:j + 1, :] = run
        if j + 1 < n_even:
            run = run + p[:, j + 1:j + 2, :]


def hgrn_lower_bounds(lb_logits):
    return pl.pallas_call(
        _lb_kernel,
        out_shape=jax.ShapeDtypeStruct(lb_logits.shape, F32),
        name="hgrn_lb",
    )(lb_logits)


def _hgrn_prep(raw_ref, lb_row, bc_ref, k_ref, tb, upper):
    lbv = lb_row
    pos = lbv > 0.0
    loglb = jnp.where(pos, jnp.log(jnp.where(pos, lbv, 1.0)), NEG_BIG)
    l1m = jnp.log1p(-lbv)
    ri = lax.broadcasted_iota(jnp.int32, (128, 128), 0)
    ci = lax.broadcasted_iota(jnp.int32, (128, 128), 1)
    same = (ri // SCAN_CHUNK) == (ci // SCAN_CHUNK)
    tri = (ci >= ri) if upper else (ci <= ri)
    lmat = jnp.where(same, jnp.where(tri, 1.0, 0.0), 0.0).astype(F32)
    lowest = None
    for rb in range(tb // 128):
        rows = pl.ds(rb * 128, 128)
        raw = raw_ref[rows, :]
        ls = jnp.minimum(raw, 0.0) - jnp.log(1.0 + jnp.exp(-jnp.abs(raw)))
        b = l1m + ls
        logf = jnp.maximum(loglb, b) + jnp.log(1.0 + jnp.exp(-jnp.abs(loglb - b)))
        k_ref[rows, :] = (1.0 - lbv) / (1.0 + jnp.exp(raw))
        bc = jnp.dot(lmat, logf, precision=HIGHEST, preferred_element_type=F32)
        bc_ref[rows, :] = bc
        low = jnp.min(bc)
        lowest = low if lowest is None else jnp.minimum(lowest, low)
    return lowest


def _hgrn_chunk(r0, q_ref, v_ref, bc_ref, k_ref, s_ref, o_ref, backward, factored):
    c = SCAN_CHUNK
    rows = pl.ds(r0, c)
    qs = _silu(q_ref[rows, :])
    v = v_ref[rows, :]
    bc = bc_ref[rows, :]
    k = k_ref[rows, :]
    bend = bc_ref[pl.ds(r0, 1), :] if backward else bc_ref[pl.ds(r0 + c - 1, 1), :]
    qd = (qs * jnp.exp(bc)).astype(BF16)
    kd = (k * jnp.exp(bend - bc)).astype(BF16)
    dec = jnp.exp(bend)
    vb = v.astype(BF16)
    if factored:
        qr = (qs * jnp.exp(bc - bend)).astype(BF16)
        ti = lax.broadcasted_iota(jnp.int32, (c, c), 0)
        si = lax.broadcasted_iota(jnp.int32, (c, c), 1)
        causal = (ti <= si) if backward else (ti >= si)
        acc = []
        for h in range(H_A):
            sl = slice(h * DK_A, (h + 1) * DK_A)
            attn = lax.dot_general(qr[:, sl], kd[:, sl], (((1,), (1,)), ((), ())), preferred_element_type=F32)
            attn = jnp.where(causal, attn, 0.0).astype(BF16)
            acc.append(jnp.dot(attn, vb[:, sl], preferred_element_type=F32))
    else:
        t_idx = lax.broadcasted_iota(jnp.int32, (c, 1), 0)
        acc = [jnp.zeros((c, DK_A), F32) for _ in range(H_A)]
        for s in range(c):
            bcs = bc_ref[pl.ds(r0 + s, 1), :]
            ks = k_ref[pl.ds(r0 + s, 1), :]
            vs = v_ref[pl.ds(r0 + s, 1), :]
            valid = (t_idx <= s) if backward else (t_idx >= s)
            e = jnp.where(valid, jnp.exp(jnp.minimum(bc - bcs, 0.0)), 0.0)
            a_full = qs * ks * e
            for h in range(H_A):
                sl = slice(h * DK_A, (h + 1) * DK_A)
                a = jnp.sum(a_full[:, sl], axis=-1, keepdims=True)
                acc[h] = acc[h] + a * vs[:, sl]
    for h in range(H_A):
        sl = slice(h * DK_A, (h + 1) * DK_A)
        st = s_ref[h]
        inter = lax.dot_general(qd[:, sl], st.astype(BF16), (((1,), (1,)), ((), ())), preferred_element_type=F32)
        upd = lax.dot_general(vb[:, sl], kd[:, sl], (((0,), (0,)), ((), ())), preferred_element_type=F32)
        s_ref[h] = st * dec[:, sl] + upd
        o_ref[rows, sl] = acc[h] + inter


def _hgrn_kernel(*refs, tb, has_init, emit_state):
    it = iter(refs)
    qf, ff, vf, qb, fb, vb, lb = [next(it) for _ in range(7)]
    if has_init:
        s0f, s0b = next(it), next(it)
    of, ob = next(it), next(it)
    if emit_state:
        sfo, sbo = next(it), next(it)
    sf, sb, bcf, kf, bcb, kb = [next(it) for _ in range(6)]
    j = pl.program_id(1)

    @pl.when(j == 0)
    def _():
        if has_init:
            sf[...] = s0f[...]
            sb[...] = s0b[...]
        else:
            sf[...] = jnp.zeros_like(sf)
            sb[...] = jnp.zeros_like(sb)

    low_f = _hgrn_prep(ff, lb[0:1, :], bcf, kf, tb, upper=False)
    low_b = _hgrn_prep(fb, lb[1:2, :], bcb, kb, tb, upper=True)
    mild = jnp.minimum(low_f, low_b) >= -SAFE_DECAY
    nchunks = tb // SCAN_CHUNK

    def scan_block(factored):
        def body(ci, carry):
            _hgrn_chunk(pl.multiple_of(ci * SCAN_CHUNK, SCAN_CHUNK), qf, vf, bcf, kf, sf, of,
                        backward=False, factored=factored)
            _hgrn_chunk(pl.multiple_of((nchunks - 1 - ci) * SCAN_CHUNK, SCAN_CHUNK), qb, vb, bcb, kb, sb, ob,
                        backward=True, factored=factored)
            return carry

        lax.fori_loop(0, nchunks, body, 0)

    @pl.when(mild)
    def _():
        scan_block(True)

    @pl.when(jnp.logical_not(mild))
    def _():
        scan_block(False)

    if emit_state:
        @pl.when(j == pl.num_programs(1) - 1)
        def _():
            sfo[...] = sf[...]
            sbo[...] = sb[...]


def hgrn_scan(h_in, lb, row0, bsz, t, init=None, emit_state=False):
    tb = min(SCAN_BLOCK, t)
    assert t % tb == 0 and row0 % tb == 0 and tb % 128 == 0
    nt = t // tb
    base = row0 // tb
    fwd = lambda c: pl.BlockSpec((tb, A_W), lambda b, j: (base + b * nt + j, c))
    bwd = lambda c: pl.BlockSpec((tb, A_W), lambda b, j: (base + b * nt + nt - 1 - j, c))
    state = pl.BlockSpec((None, H_A, DK_A, DK_A), lambda b, j: (b, 0, 0, 0))
    in_specs = [fwd(0), fwd(1), fwd(3), bwd(0), bwd(2), bwd(3), pl.BlockSpec((2, A_W), lambda b, j: (0, 0))]
    args = [h_in] * 6 + [lb]
    if init is not None:
        in_specs += [state, state]
        args += list(init)
    out_specs = [pl.BlockSpec((tb, A_W), lambda b, j: (b * nt + j, 0)),
                 pl.BlockSpec((tb, A_W), lambda b, j: (b * nt + nt - 1 - j, 0))]
    out_shape = [jax.ShapeDtypeStruct((bsz * t, A_W), F32)] * 2
    if emit_state:
        out_specs += [state, state]
        out_shape += [jax.ShapeDtypeStruct((bsz, H_A, DK_A, DK_A), F32)] * 2
    scratch = [pltpu.VMEM((H_A, DK_A, DK_A), F32)] * 2 + [pltpu.VMEM((tb, A_W), F32)] * 4
    return pl.pallas_call(
        functools.partial(_hgrn_kernel, tb=tb, has_init=init is not None, emit_state=emit_state),
        grid=(bsz, nt),
        in_specs=in_specs,
        out_specs=out_specs,
        out_shape=out_shape,
        scratch_shapes=scratch,
        compiler_params=_params(("parallel", "arbitrary")),
        name="hgrn_scan",
    )(*args)


def _even_post_kernel(of_ref, ob_ref, g_ref, u_ref, v_ref, gn_ref, ws_ref, bs_ref, lg_ref, lb_ref, o_ref):
    o = of_ref[...] + ob_ref[...]
    gate = _silu(g_ref[...])
    gn = gn_ref[...]
    for h in range(H_A):
        sl = slice(h * DK_A, (h + 1) * DK_A)
        oh = o[:, sl]
        r = lax.rsqrt(jnp.mean(oh * oh, axis=-1, keepdims=True) + RMS_EPS)
        o_ref[:, sl] = (oh * r * gn * gate[:, sl]).astype(o_ref.dtype)
    u = _gelu_tanh(u_ref[...])
    v = _gelu_tanh(v_ref[...])
    mu = jnp.mean(v, axis=-1, keepdims=True)
    vc = v - mu
    var = jnp.mean(vc * vc, axis=-1, keepdims=True)
    vn = (vc * lax.rsqrt(var + LN_EPS) * lg_ref[...] + lb_ref[...]).astype(BF16)
    gd = B_W // B_GROUPS
    for g in range(B_GROUPS):
        sl = slice(g * gd, (g + 1) * gd)
        mixed = jnp.dot(ws_ref[g].astype(BF16), vn[:, sl], preferred_element_type=F32) + bs_ref[:, sl]
        o_ref[:, A_W + g * gd:A_W + (g + 1) * gd] = (u[:, sl] * mixed).astype(o_ref.dtype)


def even_post(o_fw, o_bw, h_in, gnorm, sgu_w, sgu_b_full, sgu_ln_g, sgu_ln_b):
    n = h_in.shape[0]
    tm = CHUNK_B
    col = lambda c: pl.BlockSpec((tm, A_W), lambda i: (i, c))
    full2 = lambda a: pl.BlockSpec(a.shape, lambda i: (0, 0))
    return pl.pallas_call(
        _even_post_kernel,
        grid=(n // tm,),
        in_specs=[col(0), col(0), col(4), col(5), col(6), full2(gnorm),
                  pl.BlockSpec(sgu_w.shape, lambda i: (0, 0, 0)), full2(sgu_b_full), full2(sgu_ln_g),
                  full2(sgu_ln_b)],
        out_specs=pl.BlockSpec((tm, A_W + B_W), lambda i: (i, 0)),
        out_shape=jax.ShapeDtypeStruct((n, A_W + B_W), BF16),
        compiler_params=_params(("parallel",)),
        name="even_post",
    )(o_fw, o_bw, h_in, h_in, h_in, gnorm, sgu_w, sgu_b_full, sgu_ln_g, sgu_ln_b)


def _conv_kernel(bg_ref, cg_ref, hc_ref, cgp_ref, hcp_ref, cgn_ref, hcn_ref, w_ref, o_ref, *, dims, tm):
    i = pl.program_id(0)
    npt = dims.n_prompt // tm
    pos = jnp.where(i < npt, i % (dims.tp // tm), (i - npt) % (dims.ts // tm))
    last = jnp.where(i < npt, dims.tp // tm - 1, dims.ts // tm - 1)
    has_prev = (pos != 0).astype(F32)
    has_next = (pos != last).astype(F32)
    z = cg_ref[...] * hc_ref[...]
    z_before = cgp_ref[7:8, :] * hcp_ref[7:8, :] * has_prev
    z_after = cgn_ref[0:1, :] * hcn_ref[0:1, :] * has_next
    row = lax.broadcasted_iota(jnp.int32, (tm, 1), 0)
    zp = jnp.where(row == 0, z_before, pltpu.roll(z, 1, 0))
    zn = jnp.where(row == tm - 1, z_after, pltpu.roll(z, tm - 1, 0))
    y = w_ref[0:1, :] * zp + w_ref[1:2, :] * z + w_ref[2:3, :] * zn
    o_ref[...] = (bg_ref[...] * y).astype(o_ref.dtype)


def short_conv(h_in, conv_w, dims):
    n = h_in.shape[0]
    tm = ROW_TILE
    sub = tm // 8
    nblk8 = n // 8
    cur = lambda c: pl.BlockSpec((tm, C_W), lambda i: (i, c))
    prev = lambda c: pl.BlockSpec((8, C_W), lambda i: (jnp.maximum(i * sub - 1, 0), c))
    nxt = lambda c: pl.BlockSpec((8, C_W), lambda i: (jnp.minimum((i + 1) * sub, nblk8 - 1), c))
    return pl.pallas_call(
        functools.partial(_conv_kernel, dims=dims, tm=tm),
        grid=(n // tm,),
        in_specs=[cur(0), cur(1), cur(2), prev(1), prev(2), nxt(1), nxt(2),
                  pl.BlockSpec(conv_w.shape, lambda i: (0, 0))],
        out_specs=pl.BlockSpec((tm, C_W), lambda i: (i, 0)),
        out_shape=jax.ShapeDtypeStruct((n, C_W), BF16),
        compiler_params=_params(("parallel",)),
        name="short_conv",
    )(h_in, h_in, h_in, h_in, h_in, h_in, h_in, conv_w)


def _ctx_attn_kernel(q_ref, k_ref, v_ref, o_ref, kc_ref, vc_ref):
    scale = HD_D ** -0.5
    for h in range(H_D):
        sl = slice(h * HD_D, (h + 1) * HD_D)
        k = k_ref[:, sl]
        v = v_ref[:, sl]
        kc_ref[h] = k
        vc_ref[h] = v
        s = lax.dot_general(q_ref[:, sl].astype(BF16), k.astype(BF16), (((1,), (1,)), ((), ())),
                            preferred_element_type=F32) * scale
        m = jnp.max(s, axis=-1, keepdims=True)
        e = jnp.exp(s - m)
        p = (e / jnp.sum(e, axis=-1, keepdims=True)).astype(BF16)
        o_ref[:, sl] = jnp.dot(p, v.astype(BF16), preferred_element_type=F32).astype(o_ref.dtype)


def context_attention(h_in, bsz, t):
    blk = lambda c: pl.BlockSpec((t, D_W), lambda b: (b, c))
    cache = pl.BlockSpec((None, H_D, t, HD_D), lambda b: (b, 0, 0, 0))
    return pl.pallas_call(
        _ctx_attn_kernel,
        grid=(bsz,),
        in_specs=[blk(3), blk(4), blk(5)],
        out_specs=[pl.BlockSpec((t, D_W), lambda b: (b, 0)), cache, cache],
        out_shape=[jax.ShapeDtypeStruct((bsz * t, D_W), BF16),
                   jax.ShapeDtypeStruct((bsz, H_D, t, HD_D), F32),
                   jax.ShapeDtypeStruct((bsz, H_D, t, HD_D), F32)],
        compiler_params=_params(("parallel",)),
        name="ctx_attn",
    )(h_in, h_in, h_in)


def _na_bias_kernel(r_ref, oh_ref, o_ref):
    o_ref[...] = jnp.dot(r_ref[...], oh_ref[...], precision=HIGHEST, preferred_element_type=F32)


def na_bias_table(rpb, rows):
    nh, ndr, ndc = rpb.shape
    wr = min(WIN_R, rows)
    col = np.arange(GRID_W)
    dc = np.clip(col[None, :] - col[:, None] + WIN_C - 1, 0, 2 * WIN_C - 2)
    onehot = (dc.reshape(1, -1) == np.arange(32).reshape(-1, 1)).astype(np.float32)
    m = nh * ndr
    mp = -(-m // 8) * 8
    r2 = jnp.zeros((mp, 32), F32).at[:m, :ndc].set(rpb.reshape(m, ndc))
    b = pl.pallas_call(
        _na_bias_kernel,
        out_shape=jax.ShapeDtypeStruct((mp, GRID_W * GRID_W), F32),
        name="na_bias",
    )(r2, jnp.asarray(onehot))
    b = b[:m].reshape(nh, ndr, GRID_W, GRID_W)
    cs = np.clip(col - WIN_C // 2, 0, GRID_W - WIN_C)
    col_ok = (col[None, :] >= cs[:, None]) & (col[None, :] < cs[:, None] + WIN_C)
    b = jnp.where(jnp.asarray(col_ok)[None, None], b, NEG_BIG)
    offs = []
    for off in range(WIN_R):
        idx = np.clip(off + np.arange(wr), 0, ndr - 1)
        offs.append(jnp.transpose(b[:, idx], (0, 2, 1, 3)).reshape(nh, GRID_W, wr * GRID_W))
    return jnp.stack(offs, axis=1)


def _na_attn_kernel(q_ref, k_ref, v_ref, kc_ref, vc_ref, bias_ref, o_ref, kb, vb, kcb, vcb, *, rows, wr):
    scale = HD_D ** -0.5
    kb[...] = k_ref[...].astype(BF16)
    vb[...] = v_ref[...].astype(BF16)
    kcb[...] = kc_ref[...].astype(BF16)
    vcb[...] = vc_ref[...].astype(BF16)
    w = GRID_W

    def body(r, carry):
        rs = jnp.clip(r - wr // 2, 0, rows - wr)
        off = rs - r + WIN_R - 1
        q = q_ref[pl.ds(pl.multiple_of(r * w, w), w), :].astype(BF16)
        win = pl.ds(pl.multiple_of(rs * w, w), wr * w)
        s_loc = lax.dot_general(q, kb[win, :], (((1,), (1,)), ((), ())), preferred_element_type=F32) * scale
        s_loc = s_loc + bias_ref[off]
        s_ctx = lax.dot_general(q, kcb[...], (((1,), (1,)), ((), ())), preferred_element_type=F32) * scale
        m = jnp.maximum(jnp.max(s_loc, axis=-1, keepdims=True), jnp.max(s_ctx, axis=-1, keepdims=True))
        e_loc = jnp.exp(s_loc - m)
        e_ctx = jnp.exp(s_ctx - m)
        inv = 1.0 / (jnp.sum(e_loc, axis=-1, keepdims=True) + jnp.sum(e_ctx, axis=-1, keepdims=True))
        o = (jnp.dot((e_loc * inv).astype(BF16), vb[win, :], preferred_element_type=F32)
             + jnp.dot((e_ctx * inv).astype(BF16), vcb[...], preferred_element_type=F32))
        o_ref[pl.ds(pl.multiple_of(r * w, w), w), :] = o.astype(o_ref.dtype)
        return carry

    lax.fori_loop(0, rows, body, 0, unroll=2)


def neighbourhood_attention(h_in, cache_k, cache_v, bias, row0, bsz, t):
    assert row0 % t == 0 and t % GRID_W == 0
    rows = t // GRID_W
    wr = min(WIN_R, rows)
    base = row0 // t
    past = cache_k.shape[2]
    qkv = lambda c: pl.BlockSpec((t, HD_D), lambda b, h: (base + b, c * H_D + h))
    ctx = pl.BlockSpec((None, None, past, HD_D), lambda b, h: (b, h, 0, 0))
    return pl.pallas_call(
        functools.partial(_na_attn_kernel, rows=rows, wr=wr),
        grid=(bsz, H_D),
        in_specs=[qkv(3), qkv(4), qkv(5), ctx, ctx,
                  pl.BlockSpec((None, WIN_R, GRID_W, wr * GRID_W), lambda b, h: (h, 0, 0, 0))],
        out_specs=pl.BlockSpec((t, HD_D), lambda b, h: (b, h)),
        out_shape=jax.ShapeDtypeStruct((bsz * t, D_W), BF16),
        scratch_shapes=[pltpu.VMEM((t, HD_D), BF16), pltpu.VMEM((t, HD_D), BF16),
                        pltpu.VMEM((past, HD_D), BF16), pltpu.VMEM((past, HD_D), BF16)],
        compiler_params=_params(("parallel", "parallel")),
        name="na_attn",
    )(h_in, h_in, h_in, cache_k, cache_v, bias)


def _router_kernel(x_ref, w_ref, idx_ref, wt_ref):
    logits = jnp.dot(x_ref[...], w_ref[...], precision=HIGHEST, preferred_element_type=F32)
    lane = lax.broadcasted_iota(jnp.int32, logits.shape, 1)
    logits = jnp.where(lane < N_EXPERTS, logits, -jnp.inf)
    m1 = jnp.max(logits, axis=-1, keepdims=True)
    i1 = jnp.min(jnp.where(logits == m1, lane, 128), axis=-1, keepdims=True)
    rest = jnp.where(lane == i1, -jnp.inf, logits)
    m2 = jnp.max(rest, axis=-1, keepdims=True)
    i2 = jnp.min(jnp.where(rest == m2, lane, 128), axis=-1, keepdims=True)
    e2 = jnp.exp(m2 - m1)
    w1 = 1.0 / (1.0 + e2)
    w2 = e2 / (1.0 + e2)
    idx_ref[...] = jnp.where(lane == 0, i1, jnp.where(lane == 1, i2, 0))
    wt_ref[...] = jnp.where(lane == 0, w1, jnp.where(lane == 1, w2, 0.0))


def moe_router(xm, router):
    n, d = xm.shape
    tm = 512
    wpad = jnp.zeros((d, 128), F32).at[:, :N_EXPERTS].set(router)
    idx, wt = pl.pallas_call(
        _router_kernel,
        grid=(n // tm,),
        in_specs=[pl.BlockSpec((tm, d), lambda i: (i, 0)), pl.BlockSpec((d, 128), lambda i: (0, 0))],
        out_specs=[pl.BlockSpec((tm, 128), lambda i: (i, 0))] * 2,
        out_shape=[jax.ShapeDtypeStruct((n, 128), jnp.int32), jax.ShapeDtypeStruct((n, 128), F32)],
        compiler_params=_params(("parallel",)),
        name="router",
    )(xm, wpad)
    return idx[:, :2], wt[:, :2]


GATHER_ROWS = 256
DMA_UNROLL = 8


def _gather_kernel(src_ref, x_hbm, o_ref, buf, sem):
    nrows = buf.shape[0]

    def copy(r):
        return pltpu.make_async_copy(x_hbm.at[pl.ds(src_ref[0, r], 1), :], buf.at[pl.ds(r, 1), :], sem)

    def start(r, c):
        copy(r).start()
        return c

    def wait(r, c):
        copy(r).wait()
        return c

    lax.fori_loop(0, nrows, start, 0, unroll=DMA_UNROLL)
    lax.fori_loop(0, nrows, wait, 0, unroll=DMA_UNROLL)
    o_ref[...] = buf[...].astype(o_ref.dtype)


def gather_rows(x, src, out_dtype):
    p = src.shape[0]
    d = x.shape[1]
    r = GATHER_ROWS
    return pl.pallas_call(
        _gather_kernel,
        grid=(p // r,),
        in_specs=[pl.BlockSpec((None, 1, r), lambda i: (i, 0, 0), memory_space=pltpu.SMEM),
                  pl.BlockSpec(memory_space=pl.ANY)],
        out_specs=pl.BlockSpec((r, d), lambda i: (i, 0)),
        out_shape=jax.ShapeDtypeStruct((p, d), out_dtype),
        scratch_shapes=[pltpu.VMEM((r, d), F32), pltpu.SemaphoreType.DMA(())],
        compiler_params=_params(("arbitrary",)),
        name="gather_rows",
    )(src.reshape(p // r, 1, r), x)


def _combine_kernel(p1_ref, p2_ref, w_ref, y_hbm, o_ref, buf1, buf2, sem1, sem2):
    nrows = buf1.shape[0]

    def copy1(r):
        return pltpu.make_async_copy(y_hbm.at[pl.ds(p1_ref[0, r], 1), :], buf1.at[pl.ds(r, 1), :], sem1)

    def copy2(r):
        return pltpu.make_async_copy(y_hbm.at[pl.ds(p2_ref[0, r], 1), :], buf2.at[pl.ds(r, 1), :], sem2)

    def start(r, c):
        copy1(r).start()
        copy2(r).start()
        return c

    def wait(r, c):
        copy1(r).wait()
        copy2(r).wait()
        return c

    lax.fori_loop(0, nrows, start, 0, unroll=DMA_UNROLL)
    lax.fori_loop(0, nrows, wait, 0, unroll=DMA_UNROLL)
    w = w_ref[...]
    o_ref[...] = w[:, 0:1] * buf1[...] + w[:, 1:2] * buf2[...]


def moe_combine(y, pos1, pos2, wt):
    n = pos1.shape[0]
    d = y.shape[1]
    r = GATHER_ROWS
    wpad = jnp.zeros((n, 128), F32).at[:, :2].set(wt)
    idx = pl.BlockSpec((None, 1, r), lambda i: (i, 0, 0), memory_space=pltpu.SMEM)
    return pl.pallas_call(
        _combine_kernel,
        grid=(n // r,),
        in_specs=[idx, idx, pl.BlockSpec((r, 128), lambda i: (i, 0)), pl.BlockSpec(memory_space=pl.ANY)],
        out_specs=pl.BlockSpec((r, d), lambda i: (i, 0)),
        out_shape=jax.ShapeDtypeStruct((n, d), F32),
        scratch_shapes=[pltpu.VMEM((r, d), F32), pltpu.VMEM((r, d), F32),
                        pltpu.SemaphoreType.DMA(()), pltpu.SemaphoreType.DMA(())],
        compiler_params=_params(("arbitrary",)),
        name="moe_combine",
    )(pos1.reshape(n // r, 1, r), pos2.reshape(n // r, 1, r), wpad, y)


MOE_TILE = 512


def moe_routing(idx, tm):
    n = idx.shape[0]
    p = 2 * n + N_EXPERTS * tm
    e_flat = idx.reshape(-1)
    onehot = (e_flat[:, None] == jnp.arange(N_EXPERTS, dtype=jnp.int32)[None, :]).astype(jnp.int32)
    csum = jnp.cumsum(onehot, axis=0)
    counts = csum[-1]
    rank = jnp.sum(onehot * (csum - 1), axis=1)
    padded = ((counts + tm - 1) // tm) * tm
    ends = jnp.cumsum(padded)
    starts = ends - padded
    pos = starts[e_flat] + rank
    src = jnp.zeros((p,), jnp.int32).at[pos].set(jnp.arange(2 * n, dtype=jnp.int32) // 2)
    tile0 = jnp.arange(p // tm, dtype=jnp.int32) * tm
    tile_expert = jnp.minimum(jnp.sum((tile0[:, None] >= ends[None, :]).astype(jnp.int32), axis=1),
                              N_EXPERTS - 1).astype(jnp.int32)
    tile_valid = (tile0 < ends[-1]).astype(jnp.int32)
    pos2 = pos.reshape(n, 2)
    return src, pos2[:, 0], pos2[:, 1], tile_expert, tile_valid


def _even_layer_mixer(xm, j, dims, states, w_in, w_out, lb, gnorm, sgu_w, sgu_b, sgu_ln_g, sgu_ln_b):
    h_in = matmul(xm, w_in, j)
    lbj = lb[:, j, :]
    t_last = lambda s: jnp.swapaxes(s, -1, -2)
    ofp, obp, sfp, sbp = hgrn_scan(h_in, lbj, 0, dims.bp, dims.tp, init=None, emit_state=True)
    ofs, obs = hgrn_scan(h_in, lbj, dims.n_prompt, dims.bs, dims.ts,
                         init=(t_last(states[0]), t_last(states[1])), emit_state=False)
    o_fw = jnp.concatenate([ofp, ofs], axis=0)
    o_bw = jnp.concatenate([obp, obs], axis=0)
    sgu_b_full = jnp.repeat(sgu_b.T, B_W // B_GROUPS, axis=1)
    mix_in = even_post(o_fw, o_bw, h_in, gnorm.reshape(1, -1), sgu_w, sgu_b_full,
                       sgu_ln_g.reshape(1, -1), sgu_ln_b.reshape(1, -1))
    return matmul(mix_in, w_out, j), (t_last(sfp), t_last(sbp))


def _odd_layer_mixer(xm, j, dims, states, w_in, w_out, conv_w, rpb):
    h_in = matmul(xm, w_in, j)
    y_c = short_conv(h_in, conv_w, dims)
    o_p, k_new, v_new = context_attention(h_in, dims.bp, dims.tp)
    bias = na_bias_table(rpb, dims.ts // GRID_W)
    o_s = neighbourhood_attention(h_in, states[0], states[1], bias, dims.n_prompt, dims.bs, dims.ts)
    mix_in = jnp.concatenate([y_c, jnp.concatenate([o_p, o_s], axis=0)], axis=1)
    return matmul(mix_in, w_out, j), (k_new, v_new)


def _dense_ffn(xm, j, w_gu, w_down):
    n = xm.shape[0]
    tm = 1024 if n % 1024 == 0 else 512
    te = jnp.zeros((n // tm,), jnp.int32)
    tv = jnp.ones((n // tm,), jnp.int32)
    h = grouped_gate_up(xm, w_gu[:, None], j, te, tv, tm)
    tm2 = 512
    te2 = jnp.zeros((n // tm2,), jnp.int32)
    tv2 = jnp.ones((n // tm2,), jnp.int32)
    return grouped_down(h, w_down[None], te2, tv2, tm2)


def _moe_ffn(xm, j, router, w_gu, w_down):
    idx, wt = moe_router(xm, router)
    tm = MOE_TILE
    src, pos1, pos2, tile_expert, tile_valid = moe_routing(idx, tm)
    xs = gather_rows(xm, src, BF16)
    h = grouped_gate_up(xs, w_gu, j, tile_expert, tile_valid, tm)
    y = grouped_down(h, w_down, tile_expert, tile_valid, tm)
    return moe_combine(y, pos1, pos2, wt)


def trunk(dims, x, cond, cached, w_mod, b_mod, ln_g, ln_b, w_in_even, w_out_even, hgrn_lb_logits, hgrn_gnorm,
          sgu_w, sgu_b, sgu_ln_g, sgu_ln_b, ffn_w_gu, ffn_w_down, w_in_odd, w_out_odd, conv_w, na_rpb,
          moe_router_w, moe_w_gu, moe_w_down):
    depth = w_mod.shape[0]
    bf = lambda a: a.astype(BF16)
    mod = modulation_table(cond, w_mod, b_mod)
    lb = hgrn_lower_bounds(hgrn_lb_logits)
    lng = ln_g.reshape(depth * 2, 1, -1)
    lnb = ln_b.reshape(depth * 2, 1, -1)
    xm = premodulate(x, mod, 0, dims)
    new_state = []
    for l in range(depth):
        j = l // 2
        if l % 2 == 0:
            mix, st = _even_layer_mixer(xm, j, dims, cached[l], w_in_even, w_out_even, lb,
                                        hgrn_gnorm[j], sgu_w[j], sgu_b[j], sgu_ln_g[j], sgu_ln_b[j])
        else:
            mix, st = _odd_layer_mixer(xm, j, dims, cached[l], w_in_odd, w_out_odd, conv_w[j], na_rpb[j])
        new_state.extend(st)
        moe = l % 2 == 1
        x, xm = resid_ln(mix, x, mod, lng, lnb, l, 0, dims, next_mod=(l, 4, 3), next_dtype=F32 if moe else BF16)
        if moe:
            y = _moe_ffn(xm, j, moe_router_w[j], moe_w_gu, bf(moe_w_down[j]))
        else:
            y = _dense_ffn(xm, j, ffn_w_gu, bf(ffn_w_down[j]))
        nxt = (l + 1, 1, 0) if l + 1 < depth else None
        x, xm = resid_ln(y, x, mod, lng, lnb, l, 1, dims, next_mod=nxt)
    return x, new_state


def kernel(x_prompt, x_sample, state_fwd_0, state_bwd_0, cache_k_1, cache_v_1, state_fwd_2, state_bwd_2, cache_k_3, cache_v_3, c, c_ctx, w_mod, b_mod, ln_g, ln_b, w_in_even, w_out_even, hgrn_lb_logits, hgrn_gnorm, sgu_w, sgu_b, sgu_ln_g, sgu_ln_b, ffn_w_gu, ffn_w_down, w_in_odd, w_out_odd, conv_w, na_rpb, moe_router, moe_w_gu, moe_w_down):
    bp, tp, d = x_prompt.shape
    bs, ts, _ = x_sample.shape
    dims = Dims(bp, tp, bs, ts)
    assert bs + 1 <= MOD_ROWS
    x = jnp.concatenate([x_prompt.reshape(bp * tp, d), x_sample.reshape(bs * ts, d)], axis=0)
    cond = jnp.zeros((MOD_ROWS, d), F32).at[0].set(c_ctx).at[1:1 + bs].set(c)
    cached = [(state_fwd_0, state_bwd_0), (cache_k_1, cache_v_1), (state_fwd_2, state_bwd_2), (cache_k_3, cache_v_3)]
    x, new_state = trunk(dims, x, cond, cached, w_mod, b_mod, ln_g, ln_b, w_in_even, w_out_even, hgrn_lb_logits,
                         hgrn_gnorm, sgu_w, sgu_b, sgu_ln_g, sgu_ln_b, ffn_w_gu, ffn_w_down, w_in_odd, w_out_odd,
                         conv_w, na_rpb, moe_router, moe_w_gu, moe_w_down)
    y_prompt = x[:bp * tp].reshape(bp, tp, d)
    y_sample = x[bp * tp:].reshape(bs, ts, d)
    return (y_prompt, y_sample, *new_state)
```

```python
import functools
from typing import NamedTuple

import numpy as np
import jax
import jax.numpy as jnp
from jax import lax
from jax.experimental import pallas as pl
from jax.experimental.pallas import tpu as pltpu

F32 = jnp.float32
BF16 = jnp.bfloat16
HIGHEST = lax.Precision.HIGHEST

D_MODEL = 2048
DEPTH = 4
GRID_W = 64
H_A = 8
DK_A = 128
A_W = H_A * DK_A
B_GROUPS = 8
B_W = 1024
CHUNK_B = 128
C_W = 1024
H_D = 8
HD_D = 128
D_W = H_D * HD_D
WIN_R = 8
WIN_C = 16
D_FF = 7168
N_EXPERTS = 8
ALPHA = (2 * DEPTH) ** 0.25
LN_EPS = 1e-5
RMS_EPS = 1e-6

MOD_ROWS = 16
SCAN_CHUNK = 32
SCAN_BLOCK = 256
SAFE_DECAY = 60.0
ROW_TILE = 256
NEG_BIG = -1e30
VMEM_LIMIT = 56 * 1024 * 1024


class Dims(NamedTuple):
    bp: int
    tp: int
    bs: int
    ts: int

    @property
    def n_prompt(self):
        return self.bp * self.tp

    @property
    def n_sample(self):
        return self.bs * self.ts

    @property
    def n(self):
        return self.bp * self.tp + self.bs * self.ts


def _params(sem, vmem=VMEM_LIMIT):
    return pltpu.CompilerParams(dimension_semantics=sem, vmem_limit_bytes=vmem)


def _sigmoid(x):
    return 1.0 / (1.0 + jnp.exp(-x))


def _silu(x):
    return x * _sigmoid(x)


def _gelu_tanh(x):
    return 0.5 * x * (1.0 + jnp.tanh(np.sqrt(2.0 / np.pi).astype(np.float32) * (x + 0.044715 * (x * x * x))))


def _mod_row(tile, tm, dims):
    r0 = tile * tm
    return jnp.where(r0 < dims.n_prompt, 0, 1 + (r0 - dims.n_prompt) // dims.ts)


def _mod_spec(layer, which, nargs):
    if nargs == 1:
        return pl.BlockSpec((None, None, MOD_ROWS, D_MODEL), lambda i: (layer, which, 0, 0))
    return pl.BlockSpec((None, None, MOD_ROWS, D_MODEL), lambda i, j: (layer, which, 0, 0))


def _mod_kernel(cond_ref, w_ref, b_ref, o_ref):
    c = cond_ref[...]
    s = _silu(c).astype(BF16)
    o_ref[...] = jnp.dot(s, w_ref[...].astype(BF16), preferred_element_type=F32) + b_ref[...]


def modulation_table(cond, w_mod, b_mod):
    depth, d, _ = w_mod.shape
    tn = 1024
    nb = d // tn
    return pl.pallas_call(
        _mod_kernel,
        grid=(depth, 6, nb),
        in_specs=[pl.BlockSpec((MOD_ROWS, d), lambda l, w, n: (0, 0)),
                  pl.BlockSpec((None, d, tn), lambda l, w, n: (l, 0, w * nb + n)),
                  pl.BlockSpec((None, 1, tn), lambda l, w, n: (l, 0, w * nb + n))],
        out_specs=pl.BlockSpec((None, None, MOD_ROWS, tn), lambda l, w, n: (l, w, 0, n)),
        out_shape=jax.ShapeDtypeStruct((depth, 6, MOD_ROWS, d), F32),
        compiler_params=_params(("parallel", "parallel", "parallel")),
        name="modulation",
    )(cond, w_mod, b_mod.reshape(depth, 1, 6 * d))


def _premod_kernel(x_ref, sc_ref, sh_ref, o_ref, *, dims, tm):
    r = _mod_row(pl.program_id(0), tm, dims)
    o_ref[...] = (x_ref[...] * (1.0 + sc_ref[pl.ds(r, 1), :]) + sh_ref[pl.ds(r, 1), :]).astype(o_ref.dtype)


def premodulate(x, mod, layer, dims):
    n, d = x.shape
    tm = 256
    return pl.pallas_call(
        functools.partial(_premod_kernel, dims=dims, tm=tm),
        grid=(n // tm,),
        in_specs=[pl.BlockSpec((tm, d), lambda i: (i, 0)), _mod_spec(layer, 1, 1), _mod_spec(layer, 0, 1)],
        out_specs=pl.BlockSpec((tm, d), lambda i: (i, 0)),
        out_shape=jax.ShapeDtypeStruct((n, d), BF16),
        compiler_params=_params(("parallel",)),
        name="premod",
    )(x, mod, mod)


def _mm_kernel(x_ref, w_ref, o_ref, wb_ref):
    @pl.when(pl.program_id(1) == 0)
    def _():
        wb_ref[...] = w_ref[...].astype(BF16)

    o_ref[...] = jnp.dot(x_ref[...], wb_ref[...], preferred_element_type=F32).astype(o_ref.dtype)


def matmul(x, w, layer, out_dtype=F32, tm=1024, tn=1024):
    n, k = x.shape
    m = w.shape[2]
    while n % tm:
        tm //= 2
    return pl.pallas_call(
        _mm_kernel,
        grid=(m // tn, n // tm),
        in_specs=[pl.BlockSpec((tm, k), lambda j, i: (i, 0)),
                  pl.BlockSpec((None, k, tn), lambda j, i: (layer, 0, j))],
        out_specs=pl.BlockSpec((tm, tn), lambda j, i: (i, j)),
        out_shape=jax.ShapeDtypeStruct((n, m), out_dtype),
        scratch_shapes=[pltpu.VMEM((k, tn), BF16)],
        compiler_params=_params(("parallel", "arbitrary")),
        name="matmul",
    )(x, w)


def _gu_kernel(te_ref, tv_ref, x_ref, wg_ref, wu_ref, o_ref, wgb_ref, wub_ref):
    i = pl.program_id(1)
    fresh = jnp.logical_or(i == 0, te_ref[i] != te_ref[jnp.maximum(i - 1, 0)])

    @pl.when(fresh)
    def _():
        wgb_ref[...] = wg_ref[...].astype(BF16)
        wub_ref[...] = wu_ref[...].astype(BF16)

    @pl.when(tv_ref[i] != 0)
    def _():
        x = x_ref[...]
        g = jnp.dot(x, wgb_ref[...], preferred_element_type=F32)
        u = jnp.dot(x, wub_ref[...], preferred_element_type=F32)
        o_ref[...] = (_silu(g) * u).astype(o_ref.dtype)

    @pl.when(tv_ref[i] == 0)
    def _():
        o_ref[...] = jnp.zeros_like(o_ref)


def grouped_gate_up(x, w_gu, layer, tile_expert, tile_valid, tm, tn=1024):
    p, d = x.shape
    f = w_gu.shape[3] // 2
    nj = f // tn
    grid_spec = pltpu.PrefetchScalarGridSpec(
        num_scalar_prefetch=2,
        grid=(nj, p // tm),
        in_specs=[pl.BlockSpec((tm, d), lambda j, i, te, tv: (i, 0)),
                  pl.BlockSpec((None, None, d, tn), lambda j, i, te, tv: (layer, te[i], 0, j)),
                  pl.BlockSpec((None, None, d, tn), lambda j, i, te, tv: (layer, te[i], 0, nj + j))],
        out_specs=pl.BlockSpec((tm, tn), lambda j, i, te, tv: (i, j)),
        scratch_shapes=[pltpu.VMEM((d, tn), BF16), pltpu.VMEM((d, tn), BF16)],
    )
    return pl.pallas_call(
        _gu_kernel,
        grid_spec=grid_spec,
        out_shape=jax.ShapeDtypeStruct((p, f), BF16),
        compiler_params=_params(("parallel", "arbitrary")),
        name="gate_up",
    )(tile_expert, tile_valid, x, w_gu, w_gu)


def _down_kernel(te_ref, tv_ref, h_ref, w_ref, o_ref):
    del te_ref
    k = pl.program_id(1)
    valid = tv_ref[pl.program_id(0)] != 0

    @pl.when(k == 0)
    def _():
        o_ref[...] = jnp.zeros_like(o_ref)

    @pl.when(valid)
    def _():
        o_ref[...] += jnp.dot(h_ref[...], w_ref[...], preferred_element_type=F32)


def grouped_down(h, w_down, tile_expert, tile_valid, tm, tk=1792):
    p, f = h.shape
    d = w_down.shape[2]
    grid_spec = pltpu.PrefetchScalarGridSpec(
        num_scalar_prefetch=2,
        grid=(p // tm, f // tk),
        in_specs=[pl.BlockSpec((tm, tk), lambda i, k, te, tv: (i, k)),
                  pl.BlockSpec((None, tk, d), lambda i, k, te, tv: (te[i], k, 0))],
        out_specs=pl.BlockSpec((tm, d), lambda i, k, te, tv: (i, 0)),
    )
    return pl.pallas_call(
        _down_kernel,
        grid_spec=grid_spec,
        out_shape=jax.ShapeDtypeStruct((p, d), F32),
        compiler_params=_params(("parallel", "arbitrary")),
        name="down",
    )(tile_expert, tile_valid, h, w_down)


def _resid_ln_kernel(*refs, dims, tm, emit_next):
    if emit_next:
        y_ref, x_ref, g_ref, lng_ref, lnb_ref, scn_ref, shn_ref, xo_ref, xm_ref = refs
    else:
        y_ref, x_ref, g_ref, lng_ref, lnb_ref, xo_ref = refs
    r = _mod_row(pl.program_id(0), tm, dims)
    z = ALPHA * x_ref[...] + g_ref[pl.ds(r, 1), :] * y_ref[...]
    mu = jnp.mean(z, axis=-1, keepdims=True)
    zc = z - mu
    var = jnp.mean(zc * zc, axis=-1, keepdims=True)
    xn = zc * lax.rsqrt(var + LN_EPS) * lng_ref[...] + lnb_ref[...]
    xo_ref[...] = xn
    if emit_next:
        xm_ref[...] = (xn * (1.0 + scn_ref[pl.ds(r, 1), :]) + shn_ref[pl.ds(r, 1), :]).astype(xm_ref.dtype)


def resid_ln(y, x, mod, ln_g, ln_b, layer, sub, dims, next_mod=None, next_dtype=BF16):
    n, d = x.shape
    tm = 256
    row = pl.BlockSpec((tm, d), lambda i: (i, 0))
    vec = pl.BlockSpec((None, 1, d), lambda i: (2 * layer + sub, 0, 0))
    in_specs = [row, row, _mod_spec(layer, 2 + 3 * sub, 1), vec, vec]
    args = [y, x, mod, ln_g, ln_b]
    out_shape = [jax.ShapeDtypeStruct((n, d), F32)]
    out_specs = [row]
    if next_mod is not None:
        nl, nsc, nsh = next_mod
        in_specs += [_mod_spec(nl, nsc, 1), _mod_spec(nl, nsh, 1)]
        args += [mod, mod]
        out_shape.append(jax.ShapeDtypeStruct((n, d), next_dtype))
        out_specs.append(row)
    res = pl.pallas_call(
        functools.partial(_resid_ln_kernel, dims=dims, tm=tm, emit_next=next_mod is not None),
        grid=(n // tm,),
        in_specs=in_specs,
        out_specs=out_specs,
        out_shape=out_shape,
        compiler_params=_params(("parallel",)),
        name="resid_ln",
    )(*args)
    return (res[0], res[1]) if next_mod is not None else (res[0], None)


def _lb_kernel(l_ref, o_ref):
    x = l_ref[...]
    n_even = x.shape[1]
    m = jnp.max(x, axis=1, keepdims=True)
    e = jnp.exp(x - m)
    p = e / jnp.sum(e, axis=1, keepdims=True)
    run = jnp.zeros_like(p[:, 0:1, :])
    for j in range(n_even):
        o_ref[:, j:j + 1, :] = run
        if j + 1 < n_even:
            run = run + p[:, j + 1:j + 2, :]


def hgrn_lower_bounds(lb_logits):
    return pl.pallas_call(
        _lb_kernel,
        out_shape=jax.ShapeDtypeStruct(lb_logits.shape, F32),
        name="hgrn_lb",
    )(lb_logits)


def _hgrn_prep(raw_ref, lb_row, bc_ref, k_ref, tb, upper):
    lbv = lb_row
    pos = lbv > 0.0
    loglb = jnp.where(pos, jnp.log(jnp.where(pos, lbv, 1.0)), NEG_BIG)
    l1m = jnp.log1p(-lbv)
    ri = lax.broadcasted_iota(jnp.int32, (128, 128), 0)
    ci = lax.broadcasted_iota(jnp.int32, (128, 128), 1)
    same = (ri // SCAN_CHUNK) == (ci // SCAN_CHUNK)
    tri = (ci >= ri) if upper else (ci <= ri)
    lmat = jnp.where(same, jnp.where(tri, 1.0, 0.0), 0.0).astype(F32)
    lowest = None
    for rb in range(tb // 128):
        rows = pl.ds(rb * 128, 128)
        raw = raw_ref[rows, :]
        ls = jnp.minimum(raw, 0.0) - jnp.log(1.0 + jnp.exp(-jnp.abs(raw)))
        b = l1m + ls
        logf = jnp.maximum(loglb, b) + jnp.log(1.0 + jnp.exp(-jnp.abs(loglb - b)))
        k_ref[rows, :] = (1.0 - lbv) / (1.0 + jnp.exp(raw))
        bc = jnp.dot(lmat, logf, precision=HIGHEST, preferred_element_type=F32)
        bc_ref[rows, :] = bc
        low = jnp.min(bc)
        lowest = low if lowest is None else jnp.minimum(lowest, low)
    return lowest


def _hgrn_chunk(r0, q_ref, v_ref, bc_ref, k_ref, s_ref, o_ref, backward, factored):
    c = SCAN_CHUNK
    rows = pl.ds(r0, c)
    qs = _silu(q_ref[rows, :])
    v = v_ref[rows, :]
    bc = bc_ref[rows, :]
    k = k_ref[rows, :]
    bend = bc_ref[pl.ds(r0, 1), :] if backward else bc_ref[pl.ds(r0 + c - 1, 1), :]
    qd = (qs * jnp.exp(bc)).astype(BF16)
    kd = (k * jnp.exp(bend - bc)).astype(BF16)
    dec = jnp.exp(bend)
    vb = v.astype(BF16)
    if factored:
        qr = (qs * jnp.exp(bc - bend)).astype(BF16)
        ti = lax.broadcasted_iota(jnp.int32, (c, c), 0)
        si = lax.broadcasted_iota(jnp.int32, (c, c), 1)
        causal = (ti <= si) if backward else (ti >= si)
        acc = []
        for h in range(H_A):
            sl = slice(h * DK_A, (h + 1) * DK_A)
            attn = lax.dot_general(qr[:, sl], kd[:, sl], (((1,), (1,)), ((), ())), preferred_element_type=F32)
            attn = jnp.where(causal, attn, 0.0).astype(BF16)
            acc.append(jnp.dot(attn, vb[:, sl], preferred_element_type=F32))
    else:
        t_idx = lax.broadcasted_iota(jnp.int32, (c, 1), 0)

        def pair_step(s, acc):
            bcs = bc_ref[pl.ds(r0 + s, 1), :]
            ks = k_ref[pl.ds(r0 + s, 1), :]
            vs = v_ref[pl.ds(r0 + s, 1), :]
            valid = (t_idx <= s) if backward else (t_idx >= s)
            e = jnp.where(valid, jnp.exp(jnp.minimum(bc - bcs, 0.0)), 0.0)
            a_full = qs * ks * e
            out = []
            for h in range(H_A):
                sl = slice(h * DK_A, (h + 1) * DK_A)
                a = jnp.sum(a_full[:, sl], axis=-1, keepdims=True)
                out.append(acc[h] + a * vs[:, sl])
            return tuple(out)

        acc = lax.fori_loop(0, c, pair_step, tuple(jnp.zeros((c, DK_A), F32) for _ in range(H_A)))
    for h in range(H_A):
        sl = slice(h * DK_A, (h + 1) * DK_A)
        st = s_ref[h]
        inter = lax.dot_general(qd[:, sl], st.astype(BF16), (((1,), (1,)), ((), ())), preferred_element_type=F32)
        upd = lax.dot_general(vb[:, sl], kd[:, sl], (((0,), (0,)), ((), ())), preferred_element_type=F32)
        s_ref[h] = st * dec[:, sl] + upd
        o_ref[rows, sl] = acc[h] + inter


def _hgrn_kernel(*refs, tb, has_init, emit_state):
    it = iter(refs)
    qf, ff, vf, qb, fb, vb, lb = [next(it) for _ in range(7)]
    if has_init:
        s0f, s0b = next(it), next(it)
    of, ob = next(it), next(it)
    if emit_state:
        sfo, sbo = next(it), next(it)
    sf, sb, bcf, kf, bcb, kb = [next(it) for _ in range(6)]
    j = pl.program_id(1)

    @pl.when(j == 0)
    def _():
        if has_init:
            sf[...] = s0f[...]
            sb[...] = s0b[...]
        else:
            sf[...] = jnp.zeros_like(sf)
            sb[...] = jnp.zeros_like(sb)

    low_f = _hgrn_prep(ff, lb[0:1, :], bcf, kf, tb, upper=False)
    low_b = _hgrn_prep(fb, lb[1:2, :], bcb, kb, tb, upper=True)
    mild = jnp.minimum(low_f, low_b) >= -SAFE_DECAY
    nchunks = tb // SCAN_CHUNK

    def scan_block(factored):
        def body(ci, carry):
            _hgrn_chunk(pl.multiple_of(ci * SCAN_CHUNK, SCAN_CHUNK), qf, vf, bcf, kf, sf, of,
                        backward=False, factored=factored)
            _hgrn_chunk(pl.multiple_of((nchunks - 1 - ci) * SCAN_CHUNK, SCAN_CHUNK), qb, vb, bcb, kb, sb, ob,
                        backward=True, factored=factored)
            return carry

        lax.fori_loop(0, nchunks, body, 0)

    @pl.when(mild)
    def _():
        scan_block(True)

    @pl.when(jnp.logical_not(mild))
    def _():
        scan_block(False)

    if emit_state:
        @pl.when(j == pl.num_programs(1) - 1)
        def _():
            sfo[...] = sf[...]
            sbo[...] = sb[...]


def hgrn_scan(h_in, lb, row0, bsz, t, init=None, emit_state=False):
    tb = min(SCAN_BLOCK, t)
    assert t % tb == 0 and row0 % tb == 0 and tb % 128 == 0
    nt = t // tb
    base = row0 // tb
    fwd = lambda c: pl.BlockSpec((tb, A_W), lambda b, j: (base + b * nt + j, c))
    bwd = lambda c: pl.BlockSpec((tb, A_W), lambda b, j: (base + b * nt + nt - 1 - j, c))
    state = pl.BlockSpec((None, H_A, DK_A, DK_A), lambda b, j: (b, 0, 0, 0))
    in_specs = [fwd(0), fwd(1), fwd(3), bwd(0), bwd(2), bwd(3), pl.BlockSpec((2, A_W), lambda b, j: (0, 0))]
    args = [h_in] * 6 + [lb]
    if init is not None:
        in_specs += [state, state]
        args += list(init)
    out_specs = [pl.BlockSpec((tb, A_W), lambda b, j: (b * nt + j, 0)),
                 pl.BlockSpec((tb, A_W), lambda b, j: (b * nt + nt - 1 - j, 0))]
    out_shape = [jax.ShapeDtypeStruct((bsz * t, A_W), F32)] * 2
    if emit_state:
        out_specs += [state, state]
        out_shape += [jax.ShapeDtypeStruct((bsz, H_A, DK_A, DK_A), F32)] * 2
    scratch = [pltpu.VMEM((H_A, DK_A, DK_A), F32)] * 2 + [pltpu.VMEM((tb, A_W), F32)] * 4
    return pl.pallas_call(
        functools.partial(_hgrn_kernel, tb=tb, has_init=init is not None, emit_state=emit_state),
        grid=(bsz, nt),
        in_specs=in_specs,
        out_specs=out_specs,
        out_shape=out_shape,
        scratch_shapes=scratch,
        compiler_params=_params(("parallel", "arbitrary")),
        name="hgrn_scan",
    )(*args)


def _even_post_kernel(of_ref, ob_ref, g_ref, u_ref, v_ref, gn_ref, ws_ref, bs_ref, lg_ref, lb_ref, o_ref):
    o = of_ref[...] + ob_ref[...]
    gate = _silu(g_ref[...])
    gn = gn_ref[...]
    for h in range(H_A):
        sl = slice(h * DK_A, (h + 1) * DK_A)
        oh = o[:, sl]
        r = lax.rsqrt(jnp.mean(oh * oh, axis=-1, keepdims=True) + RMS_EPS)
        o_ref[:, sl] = (oh * r * gn * gate[:, sl]).astype(o_ref.dtype)
    u = _gelu_tanh(u_ref[...])
    v = _gelu_tanh(v_ref[...])
    mu = jnp.mean(v, axis=-1, keepdims=True)
    vc = v - mu
    var = jnp.mean(vc * vc, axis=-1, keepdims=True)
    vn = (vc * lax.rsqrt(var + LN_EPS) * lg_ref[...] + lb_ref[...]).astype(BF16)
    gd = B_W // B_GROUPS
    for g in range(B_GROUPS):
        sl = slice(g * gd, (g + 1) * gd)
        mixed = jnp.dot(ws_ref[g].astype(BF16), vn[:, sl], preferred_element_type=F32) + bs_ref[:, sl]
        o_ref[:, A_W + g * gd:A_W + (g + 1) * gd] = (u[:, sl] * mixed).astype(o_ref.dtype)


def even_post(o_fw, o_bw, h_in, gnorm, sgu_w, sgu_b_full, sgu_ln_g, sgu_ln_b):
    n = h_in.shape[0]
    tm = CHUNK_B
    col = lambda c: pl.BlockSpec((tm, A_W), lambda i: (i, c))
    full2 = lambda a: pl.BlockSpec(a.shape, lambda i: (0, 0))
    return pl.pallas_call(
        _even_post_kernel,
        grid=(n // tm,),
        in_specs=[col(0), col(0), col(4), col(5), col(6), full2(gnorm),
                  pl.BlockSpec(sgu_w.shape, lambda i: (0, 0, 0)), full2(sgu_b_full), full2(sgu_ln_g),
                  full2(sgu_ln_b)],
        out_specs=pl.BlockSpec((tm, A_W + B_W), lambda i: (i, 0)),
        out_shape=jax.ShapeDtypeStruct((n, A_W + B_W), BF16),
        compiler_params=_params(("parallel",)),
        name="even_post",
    )(o_fw, o_bw, h_in, h_in, h_in, gnorm, sgu_w, sgu_b_full, sgu_ln_g, sgu_ln_b)


def _conv_kernel(bg_ref, cg_ref, hc_ref, cgp_ref, hcp_ref, cgn_ref, hcn_ref, w_ref, o_ref, *, dims, tm):
    i = pl.program_id(0)
    npt = dims.n_prompt // tm
    pos = jnp.where(i < npt, i % (dims.tp // tm), (i - npt) % (dims.ts // tm))
    last = jnp.where(i < npt, dims.tp // tm - 1, dims.ts // tm - 1)
    has_prev = (pos != 0).astype(F32)
    has_next = (pos != last).astype(F32)
    z = cg_ref[...] * hc_ref[...]
    z_before = cgp_ref[7:8, :] * hcp_ref[7:8, :] * has_prev
    z_after = cgn_ref[0:1, :] * hcn_ref[0:1, :] * has_next
    row = lax.broadcasted_iota(jnp.int32, (tm, 1), 0)
    zp = jnp.where(row == 0, z_before, pltpu.roll(z, 1, 0))
    zn = jnp.where(row == tm - 1, z_after, pltpu.roll(z, tm - 1, 0))
    y = w_ref[0:1, :] * zp + w_ref[1:2, :] * z + w_ref[2:3, :] * zn
    o_ref[...] = (bg_ref[...] * y).astype(o_ref.dtype)


def short_conv(h_in, conv_w, dims):
    n = h_in.shape[0]
    tm = ROW_TILE
    sub = tm // 8
    nblk8 = n // 8
    cur = lambda c: pl.BlockSpec((tm, C_W), lambda i: (i, c))
    prev = lambda c: pl.BlockSpec((8, C_W), lambda i: (jnp.maximum(i * sub - 1, 0), c))
    nxt = lambda c: pl.BlockSpec((8, C_W), lambda i: (jnp.minimum((i + 1) * sub, nblk8 - 1), c))
    return pl.pallas_call(
        functools.partial(_conv_kernel, dims=dims, tm=tm),
        grid=(n // tm,),
        in_specs=[cur(0), cur(1), cur(2), prev(1), prev(2), nxt(1), nxt(2),
                  pl.BlockSpec(conv_w.shape, lambda i: (0, 0))],
        out_specs=pl.BlockSpec((tm, C_W), lambda i: (i, 0)),
        out_shape=jax.ShapeDtypeStruct((n, C_W), BF16),
        compiler_params=_params(("parallel",)),
        name="short_conv",
    )(h_in, h_in, h_in, h_in, h_in, h_in, h_in, conv_w)


def _ctx_attn_kernel(q_ref, k_ref, v_ref, o_ref, kc_ref, vc_ref):
    scale = HD_D ** -0.5
    for h in range(H_D):
        sl = slice(h * HD_D, (h + 1) * HD_D)
        k = k_ref[:, sl]
        v = v_ref[:, sl]
        kc_ref[h] = k
        vc_ref[h] = v
        s = lax.dot_general(q_ref[:, sl].astype(BF16), k.astype(BF16), (((1,), (1,)), ((), ())),
                            preferred_element_type=F32) * scale
        m = jnp.max(s, axis=-1, keepdims=True)
        e = jnp.exp(s - m)
        p = (e / jnp.sum(e, axis=-1, keepdims=True)).astype(BF16)
        o_ref[:, sl] = jnp.dot(p, v.astype(BF16), preferred_element_type=F32).astype(o_ref.dtype)


def context_attention(h_in, bsz, t):
    blk = lambda c: pl.BlockSpec((t, D_W), lambda b: (b, c))
    cache = pl.BlockSpec((None, H_D, t, HD_D), lambda b: (b, 0, 0, 0))
    return pl.pallas_call(
        _ctx_attn_kernel,
        grid=(bsz,),
        in_specs=[blk(3), blk(4), blk(5)],
        out_specs=[pl.BlockSpec((t, D_W), lambda b: (b, 0)), cache, cache],
        out_shape=[jax.ShapeDtypeStruct((bsz * t, D_W), BF16),
                   jax.ShapeDtypeStruct((bsz, H_D, t, HD_D), F32),
                   jax.ShapeDtypeStruct((bsz, H_D, t, HD_D), F32)],
        compiler_params=_params(("parallel",)),
        name="ctx_attn",
    )(h_in, h_in, h_in)


def _na_bias_kernel(r_ref, oh_ref, o_ref):
    o_ref[...] = jnp.dot(r_ref[...], oh_ref[...], precision=HIGHEST, preferred_element_type=F32)


def na_bias_table(rpb, rows):
    nh, ndr, ndc = rpb.shape
    wr = min(WIN_R, rows)
    col = np.arange(GRID_W)
    dc = np.clip(col[None, :] - col[:, None] + WIN_C - 1, 0, 2 * WIN_C - 2)
    onehot = (dc.reshape(1, -1) == np.arange(32).reshape(-1, 1)).astype(np.float32)
    m = nh * ndr
    mp = -(-m // 8) * 8
    r2 = jnp.zeros((mp, 32), F32).at[:m, :ndc].set(rpb.reshape(m, ndc))
    b = pl.pallas_call(
        _na_bias_kernel,
        out_shape=jax.ShapeDtypeStruct((mp, GRID_W * GRID_W), F32),
        name="na_bias",
    )(r2, jnp.asarray(onehot))
    b = b[:m].reshape(nh, ndr, GRID_W, GRID_W)
    cs = np.clip(col - WIN_C // 2, 0, GRID_W - WIN_C)
    col_ok = (col[None, :] >= cs[:, None]) & (col[None, :] < cs[:, None] + WIN_C)
    b = jnp.where(jnp.asarray(col_ok)[None, None], b, NEG_BIG)
    offs = []
    for off in range(WIN_R):
        idx = np.clip(off + np.arange(wr), 0, ndr - 1)
        offs.append(jnp.transpose(b[:, idx], (0, 2, 1, 3)).reshape(nh, GRID_W, wr * GRID_W))
    return jnp.stack(offs, axis=1)


def _na_attn_kernel(q_ref, k_ref, v_ref, kc_ref, vc_ref, bias_ref, o_ref, kb, vb, kcb, vcb, *, rows, wr):
    scale = HD_D ** -0.5
    kb[...] = k_ref[...].astype(BF16)
    vb[...] = v_ref[...].astype(BF16)
    kcb[...] = kc_ref[...].astype(BF16)
    vcb[...] = vc_ref[...].astype(BF16)
    w = GRID_W

    def body(r, carry):
        rs = jnp.clip(r - wr // 2, 0, rows - wr)
        off = rs - r + WIN_R - 1
        q = q_ref[pl.ds(pl.multiple_of(r * w, w), w), :].astype(BF16)
        win = pl.ds(pl.multiple_of(rs * w, w), wr * w)
        s_loc = lax.dot_general(q, kb[win, :], (((1,), (1,)), ((), ())), preferred_element_type=F32) * scale
        s_loc = s_loc + bias_ref[off]
        s_ctx = lax.dot_general(q, kcb[...], (((1,), (1,)), ((), ())), preferred_element_type=F32) * scale
        m = jnp.maximum(jnp.max(s_loc, axis=-1, keepdims=True), jnp.max(s_ctx, axis=-1, keepdims=True))
        e_loc = jnp.exp(s_loc - m)
        e_ctx = jnp.exp(s_ctx - m)
        inv = 1.0 / (jnp.sum(e_loc, axis=-1, keepdims=True) + jnp.sum(e_ctx, axis=-1, keepdims=True))
        o = (jnp.dot((e_loc * inv).astype(BF16), vb[win, :], preferred_element_type=F32)
             + jnp.dot((e_ctx * inv).astype(BF16), vcb[...], preferred_element_type=F32))
        o_ref[pl.ds(pl.multiple_of(r * w, w), w), :] = o.astype(o_ref.dtype)
        return carry

    lax.fori_loop(0, rows, body, 0, unroll=4)


def neighbourhood_attention(h_in, cache_k, cache_v, bias, row0, bsz, t):
    assert row0 % t == 0 and t % GRID_W == 0
    rows = t // GRID_W
    wr = min(WIN_R, rows)
    base = row0 // t
    past = cache_k.shape[2]
    qkv = lambda c: pl.BlockSpec((t, HD_D), lambda b, h: (base + b, c * H_D + h))
    ctx = pl.BlockSpec((None, None, past, HD_D), lambda b, h: (b, h, 0, 0))
    return pl.pallas_call(
        functools.partial(_na_attn_kernel, rows=rows, wr=wr),
        grid=(bsz, H_D),
        in_specs=[qkv(3), qkv(4), qkv(5), ctx, ctx,
                  pl.BlockSpec((None, WIN_R, GRID_W, wr * GRID_W), lambda b, h: (h, 0, 0, 0))],
        out_specs=pl.BlockSpec((t, HD_D), lambda b, h: (b, h)),
        out_shape=jax.ShapeDtypeStruct((bsz * t, D_W), BF16),
        scratch_shapes=[pltpu.VMEM((t, HD_D), BF16), pltpu.VMEM((t, HD_D), BF16),
                        pltpu.VMEM((past, HD_D), BF16), pltpu.VMEM((past, HD_D), BF16)],
        compiler_params=_params(("parallel", "parallel")),
        name="na_attn",
    )(h_in, h_in, h_in, cache_k, cache_v, bias)


def _router_kernel(x_ref, w_ref, idx_ref, wt_ref):
    logits = jnp.dot(x_ref[...], w_ref[...], precision=HIGHEST, preferred_element_type=F32)
    lane = lax.broadcasted_iota(jnp.int32, logits.shape, 1)
    logits = jnp.where(lane < N_EXPERTS, logits, -jnp.inf)
    m1 = jnp.max(logits, axis=-1, keepdims=True)
    i1 = jnp.min(jnp.where(logits == m1, lane, 128), axis=-1, keepdims=True)
    rest = jnp.where(lane == i1, -jnp.inf, logits)
    m2 = jnp.max(rest, axis=-1, keepdims=True)
    i2 = jnp.min(jnp.where(rest == m2, lane, 128), axis=-1, keepdims=True)
    e2 = jnp.exp(m2 - m1)
    w1 = 1.0 / (1.0 + e2)
    w2 = e2 / (1.0 + e2)
    idx_ref[...] = jnp.where(lane == 0, i1, jnp.where(lane == 1, i2, 0))
    wt_ref[...] = jnp.where(lane == 0, w1, jnp.where(lane == 1, w2, 0.0))


def moe_router(xm, router):
    n, d = xm.shape
    tm = 512
    wpad = jnp.zeros((d, 128), F32).at[:, :N_EXPERTS].set(router)
    idx, wt = pl.pallas_call(
        _router_kernel,
        grid=(n // tm,),
        in_specs=[pl.BlockSpec((tm, d), lambda i: (i, 0)), pl.BlockSpec((d, 128), lambda i: (0, 0))],
        out_specs=[pl.BlockSpec((tm, 128), lambda i: (i, 0))] * 2,
        out_shape=[jax.ShapeDtypeStruct((n, 128), jnp.int32), jax.ShapeDtypeStruct((n, 128), F32)],
        compiler_params=_params(("parallel",)),
        name="router",
    )(xm, wpad)
    return idx[:, :2], wt[:, :2]


GATHER_ROWS = 256
DMA_UNROLL = 8


def _gather_kernel(src_ref, x_hbm, o_ref, buf, sem):
    nrows = buf.shape[0]

    def copy(r):
        return pltpu.make_async_copy(x_hbm.at[pl.ds(src_ref[0, r], 1), :], buf.at[pl.ds(r, 1), :], sem)

    def start(r, c):
        copy(r).start()
        return c

    def wait(r, c):
        copy(r).wait()
        return c

    lax.fori_loop(0, nrows, start, 0, unroll=DMA_UNROLL)
    lax.fori_loop(0, nrows, wait, 0, unroll=DMA_UNROLL)
    o_ref[...] = buf[...].astype(o_ref.dtype)


def gather_rows(x, src, out_dtype):
    p = src.shape[0]
    d = x.shape[1]
    r = GATHER_ROWS
    return pl.pallas_call(
        _gather_kernel,
        grid=(p // r,),
        in_specs=[pl.BlockSpec((None, 1, r), lambda i: (i, 0, 0), memory_space=pltpu.SMEM),
                  pl.BlockSpec(memory_space=pl.ANY)],
        out_specs=pl.BlockSpec((r, d), lambda i: (i, 0)),
        out_shape=jax.ShapeDtypeStruct((p, d), out_dtype),
        scratch_shapes=[pltpu.VMEM((r, d), F32), pltpu.SemaphoreType.DMA(())],
        compiler_params=_params(("arbitrary",)),
        name="gather_rows",
    )(src.reshape(p // r, 1, r), x)


def _combine_kernel(p1_ref, p2_ref, w_ref, y_hbm, o_ref, buf1, buf2, sem1, sem2):
    nrows = buf1.shape[0]

    def copy1(r):
        return pltpu.make_async_copy(y_hbm.at[pl.ds(p1_ref[0, r], 1), :], buf1.at[pl.ds(r, 1), :], sem1)

    def copy2(r):
        return pltpu.make_async_copy(y_hbm.at[pl.ds(p2_ref[0, r], 1), :], buf2.at[pl.ds(r, 1), :], sem2)

    def start(r, c):
        copy1(r).start()
        copy2(r).start()
        return c

    def wait(r, c):
        copy1(r).wait()
        copy2(r).wait()
        return c

    lax.fori_loop(0, nrows, start, 0, unroll=DMA_UNROLL)
    lax.fori_loop(0, nrows, wait, 0, unroll=DMA_UNROLL)
    w = w_ref[...]
    o_ref[...] = w[:, 0:1] * buf1[...] + w[:, 1:2] * buf2[...]


def moe_combine(y, pos1, pos2, wt):
    n = pos1.shape[0]
    d = y.shape[1]
    r = GATHER_ROWS
    wpad = jnp.zeros((n, 128), F32).at[:, :2].set(wt)
    idx = pl.BlockSpec((None, 1, r), lambda i: (i, 0, 0), memory_space=pltpu.SMEM)
    return pl.pallas_call(
        _combine_kernel,
        grid=(n // r,),
        in_specs=[idx, idx, pl.BlockSpec((r, 128), lambda i: (i, 0)), pl.BlockSpec(memory_space=pl.ANY)],
        out_specs=pl.BlockSpec((r, d), lambda i: (i, 0)),
        out_shape=jax.ShapeDtypeStruct((n, d), F32),
        scratch_shapes=[pltpu.VMEM((r, d), F32), pltpu.VMEM((r, d), F32),
                        pltpu.SemaphoreType.DMA(()), pltpu.SemaphoreType.DMA(())],
        compiler_params=_params(("arbitrary",)),
        name="moe_combine",
    )(pos1.reshape(n // r, 1, r), pos2.reshape(n // r, 1, r), wpad, y)


MOE_TILE = 512


def moe_routing(idx, tm):
    n = idx.shape[0]
    p = 2 * n + N_EXPERTS * tm
    e_flat = idx.reshape(-1)
    onehot = (e_flat[:, None] == jnp.arange(N_EXPERTS, dtype=jnp.int32)[None, :]).astype(jnp.int32)
    csum = jnp.cumsum(onehot, axis=0)
    counts = csum[-1]
    rank = jnp.sum(onehot * (csum - 1), axis=1)
    padded = ((counts + tm - 1) // tm) * tm
    ends = jnp.cumsum(padded)
    starts = ends - padded
    pos = starts[e_flat] + rank
    src = jnp.zeros((p,), jnp.int32).at[pos].set(jnp.arange(2 * n, dtype=jnp.int32) // 2)
    tile0 = jnp.arange(p // tm, dtype=jnp.int32) * tm
    tile_expert = jnp.minimum(jnp.sum((tile0[:, None] >= ends[None, :]).astype(jnp.int32), axis=1),
                              N_EXPERTS - 1).astype(jnp.int32)
    tile_valid = (tile0 < ends[-1]).astype(jnp.int32)
    pos2 = pos.reshape(n, 2)
    return src, pos2[:, 0], pos2[:, 1], tile_expert, tile_valid


def _even_layer_mixer(xm, j, dims, states, w_in, w_out, lb, gnorm, sgu_w, sgu_b, sgu_ln_g, sgu_ln_b):
    h_in = matmul(xm, w_in, j)
    lbj = lb[:, j, :]
    t_last = lambda s: jnp.swapaxes(s, -1, -2)
    ofp, obp, sfp, sbp = hgrn_scan(h_in, lbj, 0, dims.bp, dims.tp, init=None, emit_state=True)
    ofs, obs = hgrn_scan(h_in, lbj, dims.n_prompt, dims.bs, dims.ts,
                         init=(t_last(states[0]), t_last(states[1])), emit_state=False)
    o_fw = jnp.concatenate([ofp, ofs], axis=0)
    o_bw = jnp.concatenate([obp, obs], axis=0)
    sgu_b_full = jnp.repeat(sgu_b.T, B_W // B_GROUPS, axis=1)
    mix_in = even_post(o_fw, o_bw, h_in, gnorm.reshape(1, -1), sgu_w, sgu_b_full,
                       sgu_ln_g.reshape(1, -1), sgu_ln_b.reshape(1, -1))
    return matmul(mix_in, w_out, j), (t_last(sfp), t_last(sbp))


def _odd_layer_mixer(xm, j, dims, states, w_in, w_out, conv_w, rpb):
    h_in = matmul(xm, w_in, j)
    y_c = short_conv(h_in, conv_w, dims)
    o_p, k_new, v_new = context_attention(h_in, dims.bp, dims.tp)
    bias = na_bias_table(rpb, dims.ts // GRID_W)
    o_s = neighbourhood_attention(h_in, states[0], states[1], bias, dims.n_prompt, dims.bs, dims.ts)
    mix_in = jnp.concatenate([y_c, jnp.concatenate([o_p, o_s], axis=0)], axis=1)
    return matmul(mix_in, w_out, j), (k_new, v_new)


def _dense_ffn(xm, j, w_gu, w_down):
    n = xm.shape[0]
    tm = 512
    te = jnp.zeros((n // tm,), jnp.int32)
    tv = jnp.ones((n // tm,), jnp.int32)
    h = grouped_gate_up(xm, w_gu[:, None], j, te, tv, tm)
    tm2 = 512
    te2 = jnp.zeros((n // tm2,), jnp.int32)
    tv2 = jnp.ones((n // tm2,), jnp.int32)
    return grouped_down(h, w_down[None], te2, tv2, tm2)


def _moe_ffn(xm, j, router, w_gu, w_down):
    idx, wt = moe_router(xm, router)
    tm = MOE_TILE
    src, pos1, pos2, tile_expert, tile_valid = moe_routing(idx, tm)
    xs = gather_rows(xm, src, BF16)
    h = grouped_gate_up(xs, w_gu, j, tile_expert, tile_valid, tm)
    y = grouped_down(h, w_down, tile_expert, tile_valid, tm)
    return moe_combine(y, pos1, pos2, wt)


def trunk(dims, x, cond, cached, w_mod, b_mod, ln_g, ln_b, w_in_even, w_out_even, hgrn_lb_logits, hgrn_gnorm,
          sgu_w, sgu_b, sgu_ln_g, sgu_ln_b, ffn_w_gu, ffn_w_down, w_in_odd, w_out_odd, conv_w, na_rpb,
          moe_router_w, moe_w_gu, moe_w_down):
    depth = w_mod.shape[0]
    bf = lambda a: a.astype(BF16)
    mod = modulation_table(cond, w_mod, b_mod)
    lb = hgrn_lower_bounds(hgrn_lb_logits)
    lng = ln_g.reshape(depth * 2, 1, -1)
    lnb = ln_b.reshape(depth * 2, 1, -1)
    xm = premodulate(x, mod, 0, dims)
    new_state = []
    for l in range(depth):
        j = l // 2
        if l % 2 == 0:
            mix, st = _even_layer_mixer(xm, j, dims, cached[l], w_in_even, w_out_even, lb,
                                        hgrn_gnorm[j], sgu_w[j], sgu_b[j], sgu_ln_g[j], sgu_ln_b[j])
        else:
            mix, st = _odd_layer_mixer(xm, j, dims, cached[l], w_in_odd, w_out_odd, conv_w[j], na_rpb[j])
        new_state.extend(st)
        moe = l % 2 == 1
        x, xm = resid_ln(mix, x, mod, lng, lnb, l, 0, dims, next_mod=(l, 4, 3), next_dtype=F32 if moe else BF16)
        if moe:
            y = _moe_ffn(xm, j, moe_router_w[j], moe_w_gu, bf(moe_w_down[j]))
        else:
            y = _dense_ffn(xm, j, ffn_w_gu, bf(ffn_w_down[j]))
        nxt = (l + 1, 1, 0) if l + 1 < depth else None
        x, xm = resid_ln(y, x, mod, lng, lnb, l, 1, dims, next_mod=nxt)
    return x, new_state


def kernel(x_prompt, x_sample, state_fwd_0, state_bwd_0, cache_k_1, cache_v_1, state_fwd_2, state_bwd_2, cache_k_3, cache_v_3, c, c_ctx, w_mod, b_mod, ln_g, ln_b, w_in_even, w_out_even, hgrn_lb_logits, hgrn_gnorm, sgu_w, sgu_b, sgu_ln_g, sgu_ln_b, ffn_w_gu, ffn_w_down, w_in_odd, w_out_odd, conv_w, na_rpb, moe_router, moe_w_gu, moe_w_down):
    bp, tp, d = x_prompt.shape
    bs, ts, _ = x_sample.shape
    dims = Dims(bp, tp, bs, ts)
    assert bs + 1 <= MOD_ROWS
    x = jnp.concatenate([x_prompt.reshape(bp * tp, d), x_sample.reshape(bs * ts, d)], axis=0)
    cond = jnp.zeros((MOD_ROWS, d), F32).at[0].set(c_ctx).at[1:1 + bs].set(c)
    cached = [(state_fwd_0, state_bwd_0), (cache_k_1, cache_v_1), (state_fwd_2, state_bwd_2), (cache_k_3, cache_v_3)]
    x, new_state = trunk(dims, x, cond, cached, w_mod, b_mod, ln_g, ln_b, w_in_even, w_out_even, hgrn_lb_logits,
                         hgrn_gnorm, sgu_w, sgu_b, sgu_ln_g, sgu_ln_b, ffn_w_gu, ffn_w_down, w_in_odd, w_out_odd,
                         conv_w, na_rpb, moe_router, moe_w_gu, moe_w_down)
    y_prompt = x[:bp * tp].reshape(bp, tp, d)
    y_sample = x[bp * tp:].reshape(bs, ts, d)
    return (y_prompt, y_sample, *new_state)
```

```python
import functools
from typing import NamedTuple

import numpy as np
import jax
import jax.numpy as jnp
from jax import lax
from jax.experimental import pallas as pl
from jax.experimental.pallas import tpu as pltpu

F32 = jnp.float32
BF16 = jnp.bfloat16
HIGHEST = lax.Precision.HIGHEST

D_MODEL = 2048
DEPTH = 4
GRID_W = 64
H_A = 8
DK_A = 128
A_W = H_A * DK_A
B_GROUPS = 8
B_W = 1024
CHUNK_B = 128
C_W = 1024
H_D = 8
HD_D = 128
D_W = H_D * HD_D
WIN_R = 8
WIN_C = 16
D_FF = 7168
N_EXPERTS = 8
ALPHA = (2 * DEPTH) ** 0.25
LN_EPS = 1e-5
RMS_EPS = 1e-6

MOD_ROWS = 16
SCAN_CHUNK = 32
SCAN_BLOCK = 256
SAFE_DECAY = 60.0
ROW_TILE = 256
NEG_BIG = -1e30
VMEM_LIMIT = 56 * 1024 * 1024


class Dims(NamedTuple):
    bp: int
    tp: int
    bs: int
    ts: int

    @property
    def n_prompt(self):
        return self.bp * self.tp

    @property
    def n_sample(self):
        return self.bs * self.ts

    @property
    def n(self):
        return self.bp * self.tp + self.bs * self.ts


def _params(sem, vmem=VMEM_LIMIT):
    return pltpu.CompilerParams(dimension_semantics=sem, vmem_limit_bytes=vmem)


def _sigmoid(x):
    return 1.0 / (1.0 + jnp.exp(-x))


def _silu(x):
    return x * _sigmoid(x)


def _gelu_tanh(x):
    return 0.5 * x * (1.0 + jnp.tanh(np.sqrt(2.0 / np.pi).astype(np.float32) * (x + 0.044715 * (x * x * x))))


def _mod_row(tile, tm, dims):
    r0 = tile * tm
    return jnp.where(r0 < dims.n_prompt, 0, 1 + (r0 - dims.n_prompt) // dims.ts)


def _mod_spec(layer, which, nargs):
    if nargs == 1:
        return pl.BlockSpec((None, None, MOD_ROWS, D_MODEL), lambda i: (layer, which, 0, 0))
    return pl.BlockSpec((None, None, MOD_ROWS, D_MODEL), lambda i, j: (layer, which, 0, 0))


def _mod_kernel(cond_ref, w_ref, b_ref, o_ref):
    c = cond_ref[...]
    s = _silu(c).astype(BF16)
    o_ref[...] = jnp.dot(s, w_ref[...].astype(BF16), preferred_element_type=F32) + b_ref[...]


def modulation_table(cond, w_mod, b_mod):
    depth, d, _ = w_mod.shape
    tn = 1024
    nb = d // tn
    return pl.pallas_call(
        _mod_kernel,
        grid=(depth, 6, nb),
        in_specs=[pl.BlockSpec((MOD_ROWS, d), lambda l, w, n: (0, 0)),
                  pl.BlockSpec((None, d, tn), lambda l, w, n: (l, 0, w * nb + n)),
                  pl.BlockSpec((None, 1, tn), lambda l, w, n: (l, 0, w * nb + n))],
        out_specs=pl.BlockSpec((None, None, MOD_ROWS, tn), lambda l, w, n: (l, w, 0, n)),
        out_shape=jax.ShapeDtypeStruct((depth, 6, MOD_ROWS, d), F32),
        compiler_params=_params(("parallel", "parallel", "parallel")),
        name="modulation",
    )(cond, w_mod, b_mod.reshape(depth, 1, 6 * d))


def _premod_kernel(x_ref, sc_ref, sh_ref, o_ref, *, dims, tm):
    r = _mod_row(pl.program_id(0), tm, dims)
    o_ref[...] = (x_ref[...] * (1.0 + sc_ref[pl.ds(r, 1), :]) + sh_ref[pl.ds(r, 1), :]).astype(o_ref.dtype)


def premodulate(x, mod, layer, dims):
    n, d = x.shape
    tm = 256
    return pl.pallas_call(
        functools.partial(_premod_kernel, dims=dims, tm=tm),
        grid=(n // tm,),
        in_specs=[pl.BlockSpec((tm, d), lambda i: (i, 0)), _mod_spec(layer, 1, 1), _mod_spec(layer, 0, 1)],
        out_specs=pl.BlockSpec((tm, d), lambda i: (i, 0)),
        out_shape=jax.ShapeDtypeStruct((n, d), BF16),
        compiler_params=_params(("parallel",)),
        name="premod",
    )(x, mod, mod)


def _mm_kernel(x_ref, w_ref, o_ref, wb_ref):
    @pl.when(pl.program_id(1) == 0)
    def _():
        wb_ref[...] = w_ref[...].astype(BF16)

    o_ref[...] = jnp.dot(x_ref[...], wb_ref[...], preferred_element_type=F32).astype(o_ref.dtype)


def matmul(x, w, layer, out_dtype=F32, tm=1024, tn=1024):
    n, k = x.shape
    m = w.shape[2]
    while n % tm:
        tm //= 2
    return pl.pallas_call(
        _mm_kernel,
        grid=(m // tn, n // tm),
        in_specs=[pl.BlockSpec((tm, k), lambda j, i: (i, 0)),
                  pl.BlockSpec((None, k, tn), lambda j, i: (layer, 0, j))],
        out_specs=pl.BlockSpec((tm, tn), lambda j, i: (i, j)),
        out_shape=jax.ShapeDtypeStruct((n, m), out_dtype),
        scratch_shapes=[pltpu.VMEM((k, tn), BF16)],
        compiler_params=_params(("parallel", "arbitrary")),
        name="matmul",
    )(x, w)


def _gu_kernel(te_ref, tv_ref, x_ref, wg_ref, wu_ref, o_ref, wgb_ref, wub_ref):
    i = pl.program_id(1)
    fresh = jnp.logical_or(i == 0, te_ref[i] != te_ref[jnp.maximum(i - 1, 0)])

    @pl.when(fresh)
    def _():
        wgb_ref[...] = wg_ref[...].astype(BF16)
        wub_ref[...] = wu_ref[...].astype(BF16)

    @pl.when(tv_ref[i] != 0)
    def _():
        x = x_ref[...]
        g = jnp.dot(x, wgb_ref[...], preferred_element_type=F32)
        u = jnp.dot(x, wub_ref[...], preferred_element_type=F32)
        o_ref[...] = (_silu(g) * u).astype(o_ref.dtype)

    @pl.when(tv_ref[i] == 0)
    def _():
        o_ref[...] = jnp.zeros_like(o_ref)


def grouped_gate_up(x, w_gu, layer, tile_expert, tile_valid, tm, tn=1024):
    p, d = x.shape
    f = w_gu.shape[3] // 2
    nj = f // tn
    grid_spec = pltpu.PrefetchScalarGridSpec(
        num_scalar_prefetch=2,
        grid=(nj, p // tm),
        in_specs=[pl.BlockSpec((tm, d), lambda j, i, te, tv: (i, 0)),
                  pl.BlockSpec((None, None, d, tn), lambda j, i, te, tv: (layer, te[i], 0, j)),
                  pl.BlockSpec((None, None, d, tn), lambda j, i, te, tv: (layer, te[i], 0, nj + j))],
        out_specs=pl.BlockSpec((tm, tn), lambda j, i, te, tv: (i, j)),
        scratch_shapes=[pltpu.VMEM((d, tn), BF16), pltpu.VMEM((d, tn), BF16)],
    )
    return pl.pallas_call(
        _gu_kernel,
        grid_spec=grid_spec,
        out_shape=jax.ShapeDtypeStruct((p, f), BF16),
        compiler_params=_params(("parallel", "arbitrary")),
        name="gate_up",
    )(tile_expert, tile_valid, x, w_gu, w_gu)


def _down_kernel(te_ref, tv_ref, h_ref, w_ref, o_ref):
    del te_ref
    k = pl.program_id(1)
    valid = tv_ref[pl.program_id(0)] != 0

    @pl.when(k == 0)
    def _():
        o_ref[...] = jnp.zeros_like(o_ref)

    @pl.when(valid)
    def _():
        o_ref[...] += jnp.dot(h_ref[...], w_ref[...], preferred_element_type=F32)


def grouped_down(h, w_down, tile_expert, tile_valid, tm, tk=1792):
    p, f = h.shape
    d = w_down.shape[2]
    grid_spec = pltpu.PrefetchScalarGridSpec(
        num_scalar_prefetch=2,
        grid=(p // tm, f // tk),
        in_specs=[pl.BlockSpec((tm, tk), lambda i, k, te, tv: (i, k)),
                  pl.BlockSpec((None, tk, d), lambda i, k, te, tv: (te[i], k, 0))],
        out_specs=pl.BlockSpec((tm, d), lambda i, k, te, tv: (i, 0)),
    )
    return pl.pallas_call(
        _down_kernel,
        grid_spec=grid_spec,
        out_shape=jax.ShapeDtypeStruct((p, d), F32),
        compiler_params=_params(("parallel", "arbitrary")),
        name="down",
    )(tile_expert, tile_valid, h, w_down)


def _resid_ln_kernel(*refs, dims, tm, emit_next):
    if emit_next:
        y_ref, x_ref, g_ref, lng_ref, lnb_ref, scn_ref, shn_ref, xo_ref, xm_ref = refs
    else:
        y_ref, x_ref, g_ref, lng_ref, lnb_ref, xo_ref = refs
    r = _mod_row(pl.program_id(0), tm, dims)
    z = ALPHA * x_ref[...] + g_ref[pl.ds(r, 1), :] * y_ref[...]
    mu = jnp.mean(z, axis=-1, keepdims=True)
    zc = z - mu
    var = jnp.mean(zc * zc, axis=-1, keepdims=True)
    xn = zc * lax.rsqrt(var + LN_EPS) * lng_ref[...] + lnb_ref[...]
    xo_ref[...] = xn
    if emit_next:
        xm_ref[...] = (xn * (1.0 + scn_ref[pl.ds(r, 1), :]) + shn_ref[pl.ds(r, 1), :]).astype(xm_ref.dtype)


def resid_ln(y, x, mod, ln_g, ln_b, layer, sub, dims, next_mod=None, next_dtype=BF16):
    n, d = x.shape
    tm = 256
    row = pl.BlockSpec((tm, d), lambda i: (i, 0))
    vec = pl.BlockSpec((None, 1, d), lambda i: (2 * layer + sub, 0, 0))
    in_specs = [row, row, _mod_spec(layer, 2 + 3 * sub, 1), vec, vec]
    args = [y, x, mod, ln_g, ln_b]
    out_shape = [jax.ShapeDtypeStruct((n, d), F32)]
    out_specs = [row]
    if next_mod is not None:
        nl, nsc, nsh = next_mod
        in_specs += [_mod_spec(nl, nsc, 1), _mod_spec(nl, nsh, 1)]
        args += [mod, mod]
        out_shape.append(jax.ShapeDtypeStruct((n, d), next_dtype))
        out_specs.append(row)
    res = pl.pallas_call(
        functools.partial(_resid_ln_kernel, dims=dims, tm=tm, emit_next=next_mod is not None),
        grid=(n // tm,),
        in_specs=in_specs,
        out_specs=out_specs,
        out_shape=out_shape,
        compiler_params=_params(("parallel",)),
        name="resid_ln",
    )(*args)
    return (res[0], res[1]) if next_mod is not None else (res[0], None)


def _lb_kernel(l_ref, o_ref):
    x = l_ref[...]
    n_even = x.shape[1]
    m = jnp.max(x, axis=1, keepdims=True)
    e = jnp.exp(x - m)
    p = e / jnp.sum(e, axis=1, keepdims=True)
    run = jnp.zeros_like(p[:, 0:1, :])
    for j in range(n_even):
        o_ref[:, j:j + 1, :] = run
        if j + 1 < n_even:
            run = run + p[:, j + 1:j + 2, :]


def hgrn_lower_bounds(lb_logits):
    return pl.pallas_call(
        _lb_kernel,
        out_shape=jax.ShapeDtypeStruct(lb_logits.shape, F32),
        name="hgrn_lb",
    )(lb_logits)


def _hgrn_prep(raw_ref, lb_row, bc_ref, k_ref, tb, upper):
    lbv = lb_row
    pos = lbv > 0.0
    loglb = jnp.where(pos, jnp.log(jnp.where(pos, lbv, 1.0)), NEG_BIG)
    l1m = jnp.log1p(-lbv)
    ri = lax.broadcasted_iota(jnp.int32, (128, 128), 0)
    ci = lax.broadcasted_iota(jnp.int32, (128, 128), 1)
    same = (ri // SCAN_CHUNK) == (ci // SCAN_CHUNK)
    tri = (ci >= ri) if upper else (ci <= ri)
    lmat = jnp.where(same, jnp.where(tri, 1.0, 0.0), 0.0).astype(F32)
    lowest = None
    for rb in range(tb // 128):
        rows = pl.ds(rb * 128, 128)
        raw = raw_ref[rows, :]
        ls = jnp.minimum(raw, 0.0) - jnp.log(1.0 + jnp.exp(-jnp.abs(raw)))
        b = l1m + ls
        logf = jnp.maximum(loglb, b) + jnp.log(1.0 + jnp.exp(-jnp.abs(loglb - b)))
        k_ref[rows, :] = (1.0 - lbv) / (1.0 + jnp.exp(raw))
        bc = jnp.dot(lmat, logf, precision=HIGHEST, preferred_element_type=F32)
        bc_ref[rows, :] = bc
        low = jnp.min(bc)
        lowest = low if lowest is None else jnp.minimum(lowest, low)
    return lowest


def _hgrn_chunk(r0, q_ref, v_ref, bc_ref, k_ref, s_ref, o_ref, backward, factored):
    c = SCAN_CHUNK
    rows = pl.ds(r0, c)
    qs = _silu(q_ref[rows, :])
    v = v_ref[rows, :]
    bc = bc_ref[rows, :]
    k = k_ref[rows, :]
    bend = bc_ref[pl.ds(r0, 1), :] if backward else bc_ref[pl.ds(r0 + c - 1, 1), :]
    qd = (qs * jnp.exp(bc)).astype(BF16)
    kd = (k * jnp.exp(bend - bc)).astype(BF16)
    dec = jnp.exp(bend)
    vb = v.astype(BF16)
    if factored:
        qr = (qs * jnp.exp(bc - bend)).astype(BF16)
        ti = lax.broadcasted_iota(jnp.int32, (c, c), 0)
        si = lax.broadcasted_iota(jnp.int32, (c, c), 1)
        causal = (ti <= si) if backward else (ti >= si)
        acc = []
        for h in range(H_A):
            sl = slice(h * DK_A, (h + 1) * DK_A)
            attn = lax.dot_general(qr[:, sl], kd[:, sl], (((1,), (1,)), ((), ())), preferred_element_type=F32)
            attn = jnp.where(causal, attn, 0.0).astype(BF16)
            acc.append(jnp.dot(attn, vb[:, sl], preferred_element_type=F32))
    else:
        t_idx = lax.broadcasted_iota(jnp.int32, (c, 1), 0)

        def pair_step(s, acc):
            bcs = bc_ref[pl.ds(r0 + s, 1), :]
            ks = k_ref[pl.ds(r0 + s, 1), :]
            vs = v_ref[pl.ds(r0 + s, 1), :]
            valid = (t_idx <= s) if backward else (t_idx >= s)
            e = jnp.where(valid, jnp.exp(jnp.minimum(bc - bcs, 0.0)), 0.0)
            a_full = qs * ks * e
            out = []
            for h in range(H_A):
                sl = slice(h * DK_A, (h + 1) * DK_A)
                a = jnp.sum(a_full[:, sl], axis=-1, keepdims=True)
                out.append(acc[h] + a * vs[:, sl])
            return tuple(out)

        acc = lax.fori_loop(0, c, pair_step, tuple(jnp.zeros((c, DK_A), F32) for _ in range(H_A)))
    for h in range(H_A):
        sl = slice(h * DK_A, (h + 1) * DK_A)
        st = s_ref[h]
        inter = lax.dot_general(qd[:, sl], st.astype(BF16), (((1,), (1,)), ((), ())), preferred_element_type=F32)
        upd = lax.dot_general(vb[:, sl], kd[:, sl], (((0,), (0,)), ((), ())), preferred_element_type=F32)
        s_ref[h] = st * dec[:, sl] + upd
        o_ref[rows, sl] = acc[h] + inter


def _hgrn_kernel(*refs, tb, has_init, emit_state):
    it = iter(refs)
    qf, ff, vf, qb, fb, vb, lb = [next(it) for _ in range(7)]
    if has_init:
        s0f, s0b = next(it), next(it)
    of, ob = next(it), next(it)
    if emit_state:
        sfo, sbo = next(it), next(it)
    sf, sb, bcf, kf, bcb, kb = [next(it) for _ in range(6)]
    j = pl.program_id(1)

    @pl.when(j == 0)
    def _():
        if has_init:
            sf[...] = s0f[...]
            sb[...] = s0b[...]
        else:
            sf[...] = jnp.zeros_like(sf)
            sb[...] = jnp.zeros_like(sb)

    low_f = _hgrn_prep(ff, lb[0:1, :], bcf, kf, tb, upper=False)
    low_b = _hgrn_prep(fb, lb[1:2, :], bcb, kb, tb, upper=True)
    mild = jnp.minimum(low_f, low_b) >= -SAFE_DECAY
    nchunks = tb // SCAN_CHUNK

    def scan_block(factored):
        def body(ci, carry):
            _hgrn_chunk(pl.multiple_of(ci * SCAN_CHUNK, SCAN_CHUNK), qf, vf, bcf, kf, sf, of,
                        backward=False, factored=factored)
            _hgrn_chunk(pl.multiple_of((nchunks - 1 - ci) * SCAN_CHUNK, SCAN_CHUNK), qb, vb, bcb, kb, sb, ob,
                        backward=True, factored=factored)
            return carry

        lax.fori_loop(0, nchunks, body, 0)

    @pl.when(mild)
    def _():
        scan_block(True)

    @pl.when(jnp.logical_not(mild))
    def _():
        scan_block(False)

    if emit_state:
        @pl.when(j == pl.num_programs(1) - 1)
        def _():
            sfo[...] = sf[...]
            sbo[...] = sb[...]


def hgrn_scan(h_in, lb, row0, bsz, t, init=None, emit_state=False):
    tb = min(SCAN_BLOCK, t)
    assert t % tb == 0 and row0 % tb == 0 and tb % 128 == 0
    nt = t // tb
    base = row0 // tb
    fwd = lambda c: pl.BlockSpec((tb, A_W), lambda b, j: (base + b * nt + j, c))
    bwd = lambda c: pl.BlockSpec((tb, A_W), lambda b, j: (base + b * nt + nt - 1 - j, c))
    state = pl.BlockSpec((None, H_A, DK_A, DK_A), lambda b, j: (b, 0, 0, 0))
    in_specs = [fwd(0), fwd(1), fwd(3), bwd(0), bwd(2), bwd(3), pl.BlockSpec((2, A_W), lambda b, j: (0, 0))]
    args = [h_in] * 6 + [lb]
    if init is not None:
        in_specs += [state, state]
        args += list(init)
    out_specs = [pl.BlockSpec((tb, A_W), lambda b, j: (b * nt + j, 0)),
                 pl.BlockSpec((tb, A_W), lambda b, j: (b * nt + nt - 1 - j, 0))]
    out_shape = [jax.ShapeDtypeStruct((bsz * t, A_W), F32)] * 2
    if emit_state:
        out_specs += [state, state]
        out_shape += [jax.ShapeDtypeStruct((bsz, H_A, DK_A, DK_A), F32)] * 2
    scratch = [pltpu.VMEM((H_A, DK_A, DK_A), F32)] * 2 + [pltpu.VMEM((tb, A_W), F32)] * 4
    return pl.pallas_call(
        functools.partial(_hgrn_kernel, tb=tb, has_init=init is not None, emit_state=emit_state),
        grid=(bsz, nt),
        in_specs=in_specs,
        out_specs=out_specs,
        out_shape=out_shape,
        scratch_shapes=scratch,
        compiler_params=_params(("parallel", "arbitrary")),
        name="hgrn_scan",
    )(*args)


def _even_post_kernel(of_ref, ob_ref, g_ref, u_ref, v_ref, gn_ref, ws_ref, bs_ref, lg_ref, lb_ref, o_ref):
    o = of_ref[...] + ob_ref[...]
    gate = _silu(g_ref[...])
    gn = gn_ref[...]
    for h in range(H_A):
        sl = slice(h * DK_A, (h + 1) * DK_A)
        oh = o[:, sl]
        r = lax.rsqrt(jnp.mean(oh * oh, axis=-1, keepdims=True) + RMS_EPS)
        o_ref[:, sl] = (oh * r * gn * gate[:, sl]).astype(o_ref.dtype)
    u = _gelu_tanh(u_ref[...])
    v = _gelu_tanh(v_ref[...])
    mu = jnp.mean(v, axis=-1, keepdims=True)
    vc = v - mu
    var = jnp.mean(vc * vc, axis=-1, keepdims=True)
    vn = (vc * lax.rsqrt(var + LN_EPS) * lg_ref[...] + lb_ref[...]).astype(BF16)
    gd = B_W // B_GROUPS
    for g in range(B_GROUPS):
        sl = slice(g * gd, (g + 1) * gd)
        mixed = jnp.dot(ws_ref[g].astype(BF16), vn[:, sl], preferred_element_type=F32) + bs_ref[:, sl]
        o_ref[:, A_W + g * gd:A_W + (g + 1) * gd] = (u[:, sl] * mixed).astype(o_ref.dtype)


def even_post(o_fw, o_bw, h_in, gnorm, sgu_w, sgu_b_full, sgu_ln_g, sgu_ln_b):
    n = h_in.shape[0]
    tm = CHUNK_B
    col = lambda c: pl.BlockSpec((tm, A_W), lambda i: (i, c))
    full2 = lambda a: pl.BlockSpec(a.shape, lambda i: (0, 0))
    return pl.pallas_call(
        _even_post_kernel,
        grid=(n // tm,),
        in_specs=[col(0), col(0), col(4), col(5), col(6), full2(gnorm),
                  pl.BlockSpec(sgu_w.shape, lambda i: (0, 0, 0)), full2(sgu_b_full), full2(sgu_ln_g),
                  full2(sgu_ln_b)],
        out_specs=pl.BlockSpec((tm, A_W + B_W), lambda i: (i, 0)),
        out_shape=jax.ShapeDtypeStruct((n, A_W + B_W), BF16),
        compiler_params=_params(("parallel",)),
        name="even_post",
    )(o_fw, o_bw, h_in, h_in, h_in, gnorm, sgu_w, sgu_b_full, sgu_ln_g, sgu_ln_b)


def _conv_kernel(bg_ref, cg_ref, hc_ref, cgp_ref, hcp_ref, cgn_ref, hcn_ref, w_ref, o_ref, *, dims, tm):
    i = pl.program_id(0)
    npt = dims.n_prompt // tm
    pos = jnp.where(i < npt, i % (dims.tp // tm), (i - npt) % (dims.ts // tm))
    last = jnp.where(i < npt, dims.tp // tm - 1, dims.ts // tm - 1)
    has_prev = (pos != 0).astype(F32)
    has_next = (pos != last).astype(F32)
    z = cg_ref[...] * hc_ref[...]
    z_before = cgp_ref[7:8, :] * hcp_ref[7:8, :] * has_prev
    z_after = cgn_ref[0:1, :] * hcn_ref[0:1, :] * has_next
    row = lax.broadcasted_iota(jnp.int32, (tm, 1), 0)
    zp = jnp.where(row == 0, z_before, pltpu.roll(z, 1, 0))
    zn = jnp.where(row == tm - 1, z_after, pltpu.roll(z, tm - 1, 0))
    y = w_ref[0:1, :] * zp + w_ref[1:2, :] * z + w_ref[2:3, :] * zn
    o_ref[...] = (bg_ref[...] * y).astype(o_ref.dtype)


def short_conv(h_in, conv_w, dims):
    n = h_in.shape[0]
    tm = ROW_TILE
    sub = tm // 8
    nblk8 = n // 8
    cur = lambda c: pl.BlockSpec((tm, C_W), lambda i: (i, c))
    prev = lambda c: pl.BlockSpec((8, C_W), lambda i: (jnp.maximum(i * sub - 1, 0), c))
    nxt = lambda c: pl.BlockSpec((8, C_W), lambda i: (jnp.minimum((i + 1) * sub, nblk8 - 1), c))
    return pl.pallas_call(
        functools.partial(_conv_kernel, dims=dims, tm=tm),
        grid=(n // tm,),
        in_specs=[cur(0), cur(1), cur(2), prev(1), prev(2), nxt(1), nxt(2),
                  pl.BlockSpec(conv_w.shape, lambda i: (0, 0))],
        out_specs=pl.BlockSpec((tm, C_W), lambda i: (i, 0)),
        out_shape=jax.ShapeDtypeStruct((n, C_W), BF16),
        compiler_params=_params(("parallel",)),
        name="short_conv",
    )(h_in, h_in, h_in, h_in, h_in, h_in, h_in, conv_w)


def _ctx_attn_kernel(q_ref, k_ref, v_ref, o_ref, kc_ref, vc_ref):
    scale = HD_D ** -0.5
    for h in range(H_D):
        sl = slice(h * HD_D, (h + 1) * HD_D)
        k = k_ref[:, sl]
        v = v_ref[:, sl]
        kc_ref[h] = k
        vc_ref[h] = v
        s = lax.dot_general(q_ref[:, sl].astype(BF16), k.astype(BF16), (((1,), (1,)), ((), ())),
                            preferred_element_type=F32) * scale
        m = jnp.max(s, axis=-1, keepdims=True)
        e = jnp.exp(s - m)
        p = (e / jnp.sum(e, axis=-1, keepdims=True)).astype(BF16)
        o_ref[:, sl] = jnp.dot(p, v.astype(BF16), preferred_element_type=F32).astype(o_ref.dtype)


def context_attention(h_in, bsz, t):
    blk = lambda c: pl.BlockSpec((t, D_W), lambda b: (b, c))
    cache = pl.BlockSpec((None, H_D, t, HD_D), lambda b: (b, 0, 0, 0))
    return pl.pallas_call(
        _ctx_attn_kernel,
        grid=(bsz,),
        in_specs=[blk(3), blk(4), blk(5)],
        out_specs=[pl.BlockSpec((t, D_W), lambda b: (b, 0)), cache, cache],
        out_shape=[jax.ShapeDtypeStruct((bsz * t, D_W), BF16),
                   jax.ShapeDtypeStruct((bsz, H_D, t, HD_D), F32),
                   jax.ShapeDtypeStruct((bsz, H_D, t, HD_D), F32)],
        compiler_params=_params(("parallel",)),
        name="ctx_attn",
    )(h_in, h_in, h_in)


def _na_bias_kernel(r_ref, oh_ref, o_ref):
    o_ref[...] = jnp.dot(r_ref[...], oh_ref[...], precision=HIGHEST, preferred_element_type=F32)


def na_bias_table(rpb, rows):
    nh, ndr, ndc = rpb.shape
    wr = min(WIN_R, rows)
    col = np.arange(GRID_W)
    dc = np.clip(col[None, :] - col[:, None] + WIN_C - 1, 0, 2 * WIN_C - 2)
    onehot = (dc.reshape(1, -1) == np.arange(32).reshape(-1, 1)).astype(np.float32)
    m = nh * ndr
    mp = -(-m // 8) * 8
    r2 = jnp.zeros((mp, 32), F32).at[:m, :ndc].set(rpb.reshape(m, ndc))
    b = pl.pallas_call(
        _na_bias_kernel,
        out_shape=jax.ShapeDtypeStruct((mp, GRID_W * GRID_W), F32),
        name="na_bias",
    )(r2, jnp.asarray(onehot))
    b = b[:m].reshape(nh, ndr, GRID_W, GRID_W)
    cs = np.clip(col - WIN_C // 2, 0, GRID_W - WIN_C)
    col_ok = (col[None, :] >= cs[:, None]) & (col[None, :] < cs[:, None] + WIN_C)
    b = jnp.where(jnp.asarray(col_ok)[None, None], b, NEG_BIG)
    offs = []
    for off in range(WIN_R):
        idx = np.clip(off + np.arange(wr), 0, ndr - 1)
        offs.append(jnp.transpose(b[:, idx], (0, 2, 1, 3)).reshape(nh, GRID_W, wr * GRID_W))
    return jnp.stack(offs, axis=1)


def _na_attn_kernel(q_ref, k_ref, v_ref, kc_ref, vc_ref, bias_ref, o_ref, kb, vb, kcb, vcb, *, rows, wr):
    scale = HD_D ** -0.5
    kb[...] = k_ref[...].astype(BF16)
    vb[...] = v_ref[...].astype(BF16)
    kcb[...] = kc_ref[...].astype(BF16)
    vcb[...] = vc_ref[...].astype(BF16)
    w = GRID_W

    def body(r, carry):
        rs = jnp.clip(r - wr // 2, 0, rows - wr)
        off = rs - r + WIN_R - 1
        q = q_ref[pl.ds(pl.multiple_of(r * w, w), w), :].astype(BF16)
        win = pl.ds(pl.multiple_of(rs * w, w), wr * w)
        s_loc = lax.dot_general(q, kb[win, :], (((1,), (1,)), ((), ())), preferred_element_type=F32) * scale
        s_loc = s_loc + bias_ref[off]
        s_ctx = lax.dot_general(q, kcb[...], (((1,), (1,)), ((), ())), preferred_element_type=F32) * scale
        m = jnp.maximum(jnp.max(s_loc, axis=-1, keepdims=True), jnp.max(s_ctx, axis=-1, keepdims=True))
        e_loc = jnp.exp(s_loc - m)
        e_ctx = jnp.exp(s_ctx - m)
        inv = 1.0 / (jnp.sum(e_loc, axis=-1, keepdims=True) + jnp.sum(e_ctx, axis=-1, keepdims=True))
        o = (jnp.dot((e_loc * inv).astype(BF16), vb[win, :], preferred_element_type=F32)
             + jnp.dot((e_ctx * inv).astype(BF16), vcb[...], preferred_element_type=F32))
        o_ref[pl.ds(pl.multiple_of(r * w, w), w), :] = o.astype(o_ref.dtype)
        return carry

    lax.fori_loop(0, rows, body, 0, unroll=4)


def neighbourhood_attention(h_in, cache_k, cache_v, bias, row0, bsz, t):
    assert row0 % t == 0 and t % GRID_W == 0
    rows = t // GRID_W
    wr = min(WIN_R, rows)
    base = row0 // t
    past = cache_k.shape[2]
    qkv = lambda c: pl.BlockSpec((t, HD_D), lambda b, h: (base + b, c * H_D + h))
    ctx = pl.BlockSpec((None, None, past, HD_D), lambda b, h: (b, h, 0, 0))
    return pl.pallas_call(
        functools.partial(_na_attn_kernel, rows=rows, wr=wr),
        grid=(bsz, H_D),
        in_specs=[qkv(3), qkv(4), qkv(5), ctx, ctx,
                  pl.BlockSpec((None, WIN_R, GRID_W, wr * GRID_W), lambda b, h: (h, 0, 0, 0))],
        out_specs=pl.BlockSpec((t, HD_D), lambda b, h: (b, h)),
        out_shape=jax.ShapeDtypeStruct((bsz * t, D_W), BF16),
        scratch_shapes=[pltpu.VMEM((t, HD_D), BF16), pltpu.VMEM((t, HD_D), BF16),
                        pltpu.VMEM((past, HD_D), BF16), pltpu.VMEM((past, HD_D), BF16)],
        compiler_params=_params(("parallel", "parallel")),
        name="na_attn",
    )(h_in, h_in, h_in, cache_k, cache_v, bias)


def _router_kernel(x_ref, w_ref, idx_ref, wt_ref):
    logits = jnp.dot(x_ref[...], w_ref[...], precision=HIGHEST, preferred_element_type=F32)
    lane = lax.broadcasted_iota(jnp.int32, logits.shape, 1)
    logits = jnp.where(lane < N_EXPERTS, logits, -jnp.inf)
    m1 = jnp.max(logits, axis=-1, keepdims=True)
    i1 = jnp.min(jnp.where(logits == m1, lane, 128), axis=-1, keepdims=True)
    rest = jnp.where(lane == i1, -jnp.inf, logits)
    m2 = jnp.max(rest, axis=-1, keepdims=True)
    i2 = jnp.min(jnp.where(rest == m2, lane, 128), axis=-1, keepdims=True)
    e2 = jnp.exp(m2 - m1)
    w1 = 1.0 / (1.0 + e2)
    w2 = e2 / (1.0 + e2)
    idx_ref[...] = jnp.where(lane == 0, i1, jnp.where(lane == 1, i2, 0))
    wt_ref[...] = jnp.where(lane == 0, w1, jnp.where(lane == 1, w2, 0.0))


def moe_router(xm, router):
    n, d = xm.shape
    tm = 512
    wpad = jnp.zeros((d, 128), F32).at[:, :N_EXPERTS].set(router)
    idx, wt = pl.pallas_call(
        _router_kernel,
        grid=(n // tm,),
        in_specs=[pl.BlockSpec((tm, d), lambda i: (i, 0)), pl.BlockSpec((d, 128), lambda i: (0, 0))],
        out_specs=[pl.BlockSpec((tm, 128), lambda i: (i, 0))] * 2,
        out_shape=[jax.ShapeDtypeStruct((n, 128), jnp.int32), jax.ShapeDtypeStruct((n, 128), F32)],
        compiler_params=_params(("parallel",)),
        name="router",
    )(xm, wpad)
    return idx[:, :2], wt[:, :2]


GATHER_ROWS = 256
DMA_UNROLL = 8


def _gather_kernel(src_ref, x_hbm, o_ref, buf, sem):
    nrows = buf.shape[0]

    def copy(r):
        return pltpu.make_async_copy(x_hbm.at[pl.ds(src_ref[0, r], 1), :], buf.at[pl.ds(r, 1), :], sem)

    def start(i, c):
        copy(2 * i).start(priority=0)
        copy(2 * i + 1).start(priority=1)
        return c

    def wait(r, c):
        copy(r).wait()
        return c

    lax.fori_loop(0, nrows // 2, start, 0, unroll=DMA_UNROLL // 2)
    lax.fori_loop(0, nrows, wait, 0, unroll=DMA_UNROLL)
    o_ref[...] = buf[...].astype(o_ref.dtype)


def gather_rows(x, src, out_dtype):
    p = src.shape[0]
    d = x.shape[1]
    r = GATHER_ROWS
    return pl.pallas_call(
        _gather_kernel,
        grid=(p // r,),
        in_specs=[pl.BlockSpec((None, 1, r), lambda i: (i, 0, 0), memory_space=pltpu.SMEM),
                  pl.BlockSpec(memory_space=pl.ANY)],
        out_specs=pl.BlockSpec((r, d), lambda i: (i, 0)),
        out_shape=jax.ShapeDtypeStruct((p, d), out_dtype),
        scratch_shapes=[pltpu.VMEM((r, d), F32), pltpu.SemaphoreType.DMA(())],
        compiler_params=_params(("arbitrary",)),
        name="gather_rows",
    )(src.reshape(p // r, 1, r), x)


def _combine_kernel(p1_ref, p2_ref, w_ref, y_hbm, o_ref, buf1, buf2, sem1, sem2):
    nrows = buf1.shape[0]

    def copy1(r):
        return pltpu.make_async_copy(y_hbm.at[pl.ds(p1_ref[0, r], 1), :], buf1.at[pl.ds(r, 1), :], sem1)

    def copy2(r):
        return pltpu.make_async_copy(y_hbm.at[pl.ds(p2_ref[0, r], 1), :], buf2.at[pl.ds(r, 1), :], sem2)

    def start(r, c):
        copy1(r).start(priority=0)
        copy2(r).start(priority=1)
        return c

    def wait(r, c):
        copy1(r).wait()
        copy2(r).wait()
        return c

    lax.fori_loop(0, nrows, start, 0, unroll=DMA_UNROLL)
    lax.fori_loop(0, nrows, wait, 0, unroll=DMA_UNROLL)
    w = w_ref[...]
    o_ref[...] = w[:, 0:1] * buf1[...] + w[:, 1:2] * buf2[...]


def moe_combine(y, pos1, pos2, wt):
    n = pos1.shape[0]
    d = y.shape[1]
    r = GATHER_ROWS
    wpad = jnp.zeros((n, 128), F32).at[:, :2].set(wt)
    idx = pl.BlockSpec((None, 1, r), lambda i: (i, 0, 0), memory_space=pltpu.SMEM)
    return pl.pallas_call(
        _combine_kernel,
        grid=(n // r,),
        in_specs=[idx, idx, pl.BlockSpec((r, 128), lambda i: (i, 0)), pl.BlockSpec(memory_space=pl.ANY)],
        out_specs=pl.BlockSpec((r, d), lambda i: (i, 0)),
        out_shape=jax.ShapeDtypeStruct((n, d), F32),
        scratch_shapes=[pltpu.VMEM((r, d), F32), pltpu.VMEM((r, d), F32),
                        pltpu.SemaphoreType.DMA(()), pltpu.SemaphoreType.DMA(())],
        compiler_params=_params(("arbitrary",)),
        name="moe_combine",
    )(pos1.reshape(n // r, 1, r), pos2.reshape(n // r, 1, r), wpad, y)


MOE_TILE = 512


def moe_routing(idx, tm):
    n = idx.shape[0]
    p = 2 * n + N_EXPERTS * tm
    e_flat = idx.reshape(-1)
    onehot = (e_flat[:, None] == jnp.arange(N_EXPERTS, dtype=jnp.int32)[None, :]).astype(jnp.int32)
    csum = jnp.cumsum(onehot, axis=0)
    counts = csum[-1]
    rank = jnp.sum(onehot * (csum - 1), axis=1)
    padded = ((counts + tm - 1) // tm) * tm
    ends = jnp.cumsum(padded)
    starts = ends - padded
    pos = starts[e_flat] + rank
    src = jnp.zeros((p,), jnp.int32).at[pos].set(jnp.arange(2 * n, dtype=jnp.int32) // 2)
    tile0 = jnp.arange(p // tm, dtype=jnp.int32) * tm
    tile_expert = jnp.minimum(jnp.sum((tile0[:, None] >= ends[None, :]).astype(jnp.int32), axis=1),
                              N_EXPERTS - 1).astype(jnp.int32)
    tile_valid = (tile0 < ends[-1]).astype(jnp.int32)
    pos2 = pos.reshape(n, 2)
    return src, pos2[:, 0], pos2[:, 1], tile_expert, tile_valid


def _even_layer_mixer(xm, j, dims, states, w_in, w_out, lb, gnorm, sgu_w, sgu_b, sgu_ln_g, sgu_ln_b):
    h_in = matmul(xm, w_in, j)
    lbj = lb[:, j, :]
    t_last = lambda s: jnp.swapaxes(s, -1, -2)
    ofp, obp, sfp, sbp = hgrn_scan(h_in, lbj, 0, dims.bp, dims.tp, init=None, emit_state=True)
    ofs, obs = hgrn_scan(h_in, lbj, dims.n_prompt, dims.bs, dims.ts,
                         init=(t_last(states[0]), t_last(states[1])), emit_state=False)
    o_fw = jnp.concatenate([ofp, ofs], axis=0)
    o_bw = jnp.concatenate([obp, obs], axis=0)
    sgu_b_full = jnp.repeat(sgu_b.T, B_W // B_GROUPS, axis=1)
    mix_in = even_post(o_fw, o_bw, h_in, gnorm.reshape(1, -1), sgu_w, sgu_b_full,
                       sgu_ln_g.reshape(1, -1), sgu_ln_b.reshape(1, -1))
    return matmul(mix_in, w_out, j), (t_last(sfp), t_last(sbp))


def _odd_layer_mixer(xm, j, dims, states, w_in, w_out, conv_w, rpb):
    h_in = matmul(xm, w_in, j)
    y_c = short_conv(h_in, conv_w, dims)
    o_p, k_new, v_new = context_attention(h_in, dims.bp, dims.tp)
    bias = na_bias_table(rpb, dims.ts // GRID_W)
    o_s = neighbourhood_attention(h_in, states[0], states[1], bias, dims.n_prompt, dims.bs, dims.ts)
    mix_in = jnp.concatenate([y_c, jnp.concatenate([o_p, o_s], axis=0)], axis=1)
    return matmul(mix_in, w_out, j), (k_new, v_new)


def _dense_ffn(xm, j, w_gu, w_down):
    n = xm.shape[0]
    tm = 512
    te = jnp.zeros((n // tm,), jnp.int32)
    tv = jnp.ones((n // tm,), jnp.int32)
    h = grouped_gate_up(xm, w_gu[:, None], j, te, tv, tm)
    tm2 = 512
    te2 = jnp.zeros((n // tm2,), jnp.int32)
    tv2 = jnp.ones((n // tm2,), jnp.int32)
    return grouped_down(h, w_down[None], te2, tv2, tm2)


def _moe_ffn(xm, j, router, w_gu, w_down):
    idx, wt = moe_router(xm, router)
    tm = MOE_TILE
    src, pos1, pos2, tile_expert, tile_valid = moe_routing(idx, tm)
    xs = gather_rows(xm, src, BF16)
    h = grouped_gate_up(xs, w_gu, j, tile_expert, tile_valid, tm)
    y = grouped_down(h, w_down, tile_expert, tile_valid, tm)
    return moe_combine(y, pos1, pos2, wt)


def trunk(dims, x, cond, cached, w_mod, b_mod, ln_g, ln_b, w_in_even, w_out_even, hgrn_lb_logits, hgrn_gnorm,
          sgu_w, sgu_b, sgu_ln_g, sgu_ln_b, ffn_w_gu, ffn_w_down, w_in_odd, w_out_odd, conv_w, na_rpb,
          moe_router_w, moe_w_gu, moe_w_down):
    depth = w_mod.shape[0]
    bf = lambda a: a.astype(BF16)
    mod = modulation_table(cond, w_mod, b_mod)
    lb = hgrn_lower_bounds(hgrn_lb_logits)
    lng = ln_g.reshape(depth * 2, 1, -1)
    lnb = ln_b.reshape(depth * 2, 1, -1)
    xm = premodulate(x, mod, 0, dims)
    new_state = []
    for l in range(depth):
        j = l // 2
        if l % 2 == 0:
            mix, st = _even_layer_mixer(xm, j, dims, cached[l], w_in_even, w_out_even, lb,
                                        hgrn_gnorm[j], sgu_w[j], sgu_b[j], sgu_ln_g[j], sgu_ln_b[j])
        else:
            mix, st = _odd_layer_mixer(xm, j, dims, cached[l], w_in_odd, w_out_odd, conv_w[j], na_rpb[j])
        new_state.extend(st)
        moe = l % 2 == 1
        x, xm = resid_ln(mix, x, mod, lng, lnb, l, 0, dims, next_mod=(l, 4, 3), next_dtype=F32 if moe else BF16)
        if moe:
            y = _moe_ffn(xm, j, moe_router_w[j], moe_w_gu, bf(moe_w_down[j]))
        else:
            y = _dense_ffn(xm, j, ffn_w_gu, bf(ffn_w_down[j]))
        nxt = (l + 1, 1, 0) if l + 1 < depth else None
        x, xm = resid_ln(y, x, mod, lng, lnb, l, 1, dims, next_mod=nxt)
    return x, new_state


def kernel(x_prompt, x_sample, state_fwd_0, state_bwd_0, cache_k_1, cache_v_1, state_fwd_2, state_bwd_2, cache_k_3, cache_v_3, c, c_ctx, w_mod, b_mod, ln_g, ln_b, w_in_even, w_out_even, hgrn_lb_logits, hgrn_gnorm, sgu_w, sgu_b, sgu_ln_g, sgu_ln_b, ffn_w_gu, ffn_w_down, w_in_odd, w_out_odd, conv_w, na_rpb, moe_router, moe_w_gu, moe_w_down):
    bp, tp, d = x_prompt.shape
    bs, ts, _ = x_sample.shape
    dims = Dims(bp, tp, bs, ts)
    assert bs + 1 <= MOD_ROWS
    x = jnp.concatenate([x_prompt.reshape(bp * tp, d), x_sample.reshape(bs * ts, d)], axis=0)
    cond = jnp.zeros((MOD_ROWS, d), F32).at[0].set(c_ctx).at[1:1 + bs].set(c)
    cached = [(state_fwd_0, state_bwd_0), (cache_k_1, cache_v_1), (state_fwd_2, state_bwd_2), (cache_k_3, cache_v_3)]
    x, new_state = trunk(dims, x, cond, cached, w_mod, b_mod, ln_g, ln_b, w_in_even, w_out_even, hgrn_lb_logits,
                         hgrn_gnorm, sgu_w, sgu_b, sgu_ln_g, sgu_ln_b, ffn_w_gu, ffn_w_down, w_in_odd, w_out_odd,
                         conv_w, na_rpb, moe_router, moe_w_gu, moe_w_down)
    y_prompt = x[:bp * tp].reshape(bp, tp, d)
    y_sample = x[bp * tp:].reshape(bs, ts, d)
    return (y_prompt, y_sample, *new_state)
```
